```python
import jax, jax.numpy as jnp
from jax import lax
import numpy as np

D_MODEL = 1024
BATCH = 16
SEQ = 4096
DEPTH = 2

N_MIXERS = 2
RMS_EPS = 1e-6
GM_CHUNK = 128
GM_WIDTH = 2 * D_MODEL
GM_GROUPS = 8
GM_GROUP_DIM = GM_WIDTH // GM_GROUPS
RET_QK_DIM = 256
RET_HEADS = D_MODEL // RET_QK_DIM
RET_V_DIM = 2 * D_MODEL // RET_HEADS
RET_CHUNK = 128
ROPE_BASE = 10000.0
D_FF = 4 * D_MODEL
N_A = (DEPTH + 1) // 2
N_B = DEPTH // 2

kernel_name = "hybrid_gmlp_retention_trunk"


def rmsnorm(x, g):
    xf = x.astype(jnp.float32)
    y = xf * lax.rsqrt(jnp.mean(xf * xf, axis=-1, keepdims=True) + RMS_EPS)
    return (y * g.astype(jnp.float32)).astype(x.dtype)


def chunked_spatial_gating(xn, w_in, b_in, v_norm_g, w_s, b_s, w_out):
    B, S, _ = xn.shape
    nc = S // GM_CHUNK
    z = jax.nn.gelu(xn @ w_in + b_in)
    u, v = jnp.split(z, 2, axis=-1)
    v = rmsnorm(v, v_norm_g)
    v = v.reshape(B, nc, GM_CHUNK, GM_GROUPS, GM_GROUP_DIM)
    causal = jnp.tril(jnp.ones((GM_CHUNK, GM_CHUNK), dtype=bool))
    w = jnp.where(causal[None], w_s, 0.0)
    s = jnp.einsum('gts,bnsgc->bntgc', w, v) + b_s.T[:, :, None]
    s = s.reshape(B, S, GM_WIDTH)
    return (u * s) @ w_out


def rotary(t, cos, sin):
    t1, t2 = jnp.split(t, 2, axis=-1)
    return jnp.concatenate([t1 * cos - t2 * sin, t2 * cos + t1 * sin], axis=-1)


def retention(xn, w_in, head_norm_g, w_out):
    B, S, _ = xn.shape
    H, dk, dv, C = RET_HEADS, RET_QK_DIM, RET_V_DIM, RET_CHUNK
    nc = S // C
    f32 = jnp.float32
    proj = xn @ w_in
    q, k, v, g = jnp.split(proj, [H * dk, 2 * H * dk, 2 * H * dk + H * dv], axis=-1)
    q = q.reshape(B, S, H, dk).astype(f32)
    k = k.reshape(B, S, H, dk).astype(f32) * (dk ** -0.5)
    v = v.reshape(B, S, H, dv).astype(f32)
    pos = jnp.arange(S, dtype=f32)
    inv_freq = 1.0 / (ROPE_BASE ** jnp.linspace(0.0, 1.0, dk // 2, dtype=f32))
    ang = pos[:, None] * inv_freq[None, :]
    cos = jnp.cos(ang)[None, :, None, :]
    sin = jnp.sin(ang)[None, :, None, :]
    q = rotary(q, cos, sin)
    k = rotary(k, cos, sin)
    log_gamma = jnp.log(1.0 - 2.0 ** (-5.0 - jnp.arange(H, dtype=f32)))
    idx = jnp.arange(C, dtype=f32)
    diff = idx[:, None] - idx[None, :]
    inner_decay = jnp.where(diff >= 0, jnp.exp(log_gamma[:, None, None] * jnp.maximum(diff, 0.0)), 0.0)
    query_decay = jnp.exp(log_gamma[:, None] * (idx + 1.0))[None, :, :, None]
    key_decay = jnp.exp(log_gamma[:, None] * (C - 1.0 - idx))[None, :, :, None]
    chunk_decay = jnp.exp(log_gamma * C)[None, :, None, None]

    def to_chunks(t):
        return t.reshape(B, nc, C, H, t.shape[-1]).transpose(1, 0, 3, 2, 4)

    def step(state, qkv):
        qi, ki, vi = qkv
        scores = jnp.einsum('bhik,bhjk->bhij', qi, ki) * inner_decay
        o = jnp.einsum('bhij,bhjv->bhiv', scores, vi)
        o = o + jnp.einsum('bhik,bhkv->bhiv', qi * query_decay, state)
        state = state * chunk_decay + jnp.einsum('bhjk,bhjv->bhkv', ki * key_decay, vi)
        return state, o

    state0 = jnp.zeros((B, H, dk, dv), f32)
    _, o = lax.scan(step, state0, (to_chunks(q), to_chunks(k), to_chunks(v)))
    o = o.transpose(1, 0, 3, 2, 4).reshape(B, S, H, dv)
    o = rmsnorm(o, head_norm_g.reshape(H, dv)).reshape(B, S, H * dv)
    o = (o * jax.nn.silu(g.astype(f32))).astype(xn.dtype)
    return o @ w_out


def squared_relu_mlp(h, w1, w2):
    a = jax.nn.relu(h @ w1)
    return (a * a) @ w2


def setup_inputs(seed: int = 0) -> dict:
    key = jax.random.key(seed)
    ks = jax.random.split(key, 16)
    f32 = jnp.float32
    nrm = lambda k, shape, scale: jax.random.normal(k, shape, f32) * scale
    return {
        "x": nrm(ks[0], (BATCH, SEQ, D_MODEL), 1.0),
        "norm_mix_g": 1.0 + nrm(ks[1], (DEPTH, D_MODEL), 0.02),
        "norm_ffn_g": 1.0 + nrm(ks[2], (DEPTH, D_MODEL), 0.02),
        "a_w_in": nrm(ks[3], (N_A, D_MODEL, 2 * GM_WIDTH), D_MODEL ** -0.5),
        "a_b_in": nrm(ks[4], (N_A, 2 * GM_WIDTH), 0.02),
        "a_v_norm_g": 1.0 + nrm(ks[5], (N_A, GM_WIDTH), 0.02),
        "a_w_s": nrm(ks[6], (N_A, GM_GROUPS, GM_CHUNK, GM_CHUNK), GM_CHUNK ** -0.5),
        "a_b_s": 1.0 + nrm(ks[7], (N_A, GM_GROUPS, GM_CHUNK), 0.02),
        "a_w_out": nrm(ks[8], (N_A, GM_WIDTH, D_MODEL), GM_WIDTH ** -0.5),
        "b_w_in": nrm(ks[9], (N_B, D_MODEL, 2 * RET_HEADS * (RET_QK_DIM + RET_V_DIM)), D_MODEL ** -0.5),
        "b_head_norm_g": 1.0 + nrm(ks[10], (N_B, RET_HEADS * RET_V_DIM), 0.02),
        "b_w_out": nrm(ks[11], (N_B, RET_HEADS * RET_V_DIM, D_MODEL), (RET_HEADS * RET_V_DIM) ** -0.5),
        "mlp_w1": nrm(ks[12], (DEPTH, D_MODEL, D_FF), D_MODEL ** -0.5),
        "mlp_w2": nrm(ks[13], (DEPTH, D_FF, D_MODEL), D_FF ** -0.5),
        "final_norm_g": 1.0 + nrm(ks[14], (D_MODEL,), 0.02),
    }


def reference(x, norm_mix_g, norm_ffn_g, a_w_in, a_b_in, a_v_norm_g, a_w_s, a_b_s, a_w_out,
              b_w_in, b_head_norm_g, b_w_out, mlp_w1, mlp_w2, final_norm_g):
    for i in range(DEPTH):
        h = rmsnorm(x, norm_mix_g[i])
        j = i // N_MIXERS
        if i % N_MIXERS == 0:
            h = chunked_spatial_gating(h, a_w_in[j], a_b_in[j], a_v_norm_g[j], a_w_s[j], a_b_s[j], a_w_out[j])
        else:
            h = retention(h, b_w_in[j], b_head_norm_g[j], b_w_out[j])
        x = x + h
        h = rmsnorm(x, norm_ffn_g[i])
        x = x + squared_relu_mlp(h, mlp_w1[i], mlp_w2[i])
    return rmsnorm(x, final_norm_g)
```

```python
import functools

import jax
import jax.numpy as jnp
from jax import lax
from jax.experimental import pallas as pl
from jax.experimental.pallas import tpu as pltpu

D_MODEL = 1024
DEPTH = 2
RMS_EPS = 1e-6
GM_CHUNK = 128
GM_WIDTH = 2 * D_MODEL
GM_GROUPS = 8
GM_GROUP_DIM = GM_WIDTH // GM_GROUPS
RET_QK_DIM = 256
RET_HEADS = D_MODEL // RET_QK_DIM
RET_V_DIM = 2 * D_MODEL // RET_HEADS
RET_CHUNK = 128
ROPE_BASE = 10000.0
D_FF = 4 * D_MODEL

F32 = jnp.float32
BF16 = jnp.bfloat16

V7X_VMEM_LIMIT_BYTES = 56 * 1024 * 1024

ROW_TILE = 512
FF_CHUNK = 1024


def _rmsnorm(x, g):
    ms = jnp.mean(x * x, axis=-1, keepdims=True)
    return x * lax.rsqrt(ms + RMS_EPS) * g


def _gelu_tanh(x):
    c = (2.0 / jnp.pi) ** 0.5
    return x * (0.5 * (1.0 + jnp.tanh(c * (x + 0.044715 * (x * x * x)))))


def _dot(a, b):
    return jnp.dot(a, b, preferred_element_type=F32)


def _resident(shape):
    zeros = (0,) * len(shape)
    return pl.BlockSpec(shape, lambda *_: zeros, pipeline_mode=pl.Buffered(1))


def _ffn_kernel(x_ref, g_ref, w1_ref, w2_ref, gf_ref, o_ref, *, final_norm):
    x = x_ref[...]
    h = _rmsnorm(x, g_ref[...]).astype(BF16)
    acc = x
    for c in range(D_FF // FF_CHUNK):
        cols = slice(c * FF_CHUNK, (c + 1) * FF_CHUNK)
        a = jnp.maximum(_dot(h, w1_ref[:, cols]), 0.0)
        acc = acc + _dot((a * a).astype(BF16), w2_ref[cols, :])
    if final_norm:
        acc = _rmsnorm(acc, gf_ref[...])
    o_ref[...] = acc


def _ffn(x2d, g, w1, w2, gf, *, final_norm):
    rows = x2d.shape[0]
    row_spec = pl.BlockSpec((ROW_TILE, D_MODEL), lambda i: (i, 0))
    return pl.pallas_call(
        functools.partial(_ffn_kernel, final_norm=final_norm),
        grid=(rows // ROW_TILE,),
        in_specs=[
            row_spec,
            _resident((1, D_MODEL)),
            _resident((D_MODEL, D_FF)),
            _resident((D_FF, D_MODEL)),
            _resident((1, D_MODEL)),
        ],
        out_specs=row_spec,
        out_shape=jax.ShapeDtypeStruct(x2d.shape, F32),
        compiler_params=pltpu.CompilerParams(
            dimension_semantics=("arbitrary",), vmem_limit_bytes=V7X_VMEM_LIMIT_BYTES),
        name="ffn_final" if final_norm else "ffn",
    )(x2d, g, w1, w2, gf)


def _gmlp_kernel(x_ref, g_ref, win_ref, bin_ref, gv_ref, ws_ref, bst_ref, wout_ref, o_ref,
                 h_ref, vn_ref, us_ref):
    x = x_ref[...]
    h_ref[...] = _rmsnorm(x, g_ref[...]).astype(BF16)

    v = _gelu_tanh(_dot(h_ref[...], win_ref[:, GM_WIDTH:]) + bin_ref[:, GM_WIDTH:])
    vn_ref[...] = _rmsnorm(v, gv_ref[...]).astype(BF16)

    t_idx = lax.broadcasted_iota(jnp.int32, (GM_CHUNK, GM_CHUNK), 0)
    s_idx = lax.broadcasted_iota(jnp.int32, (GM_CHUNK, GM_CHUNK), 1)
    causal = s_idx <= t_idx
    for grp in range(GM_GROUPS):
        cols = slice(grp * GM_GROUP_DIM, (grp + 1) * GM_GROUP_DIM)
        w_causal = jnp.where(causal, ws_ref[grp], 0.0).astype(BF16)
        bias = bst_ref[:, grp:grp + 1]
        u = _gelu_tanh(_dot(h_ref[...], win_ref[:, cols]) + bin_ref[:, cols])
        for c in range(ROW_TILE // GM_CHUNK):
            rows = slice(c * GM_CHUNK, (c + 1) * GM_CHUNK)
            s = _dot(w_causal, vn_ref[rows, cols]) + bias
            us_ref[rows, cols] = (u[rows] * s).astype(BF16)
    o_ref[...] = x + _dot(us_ref[...], wout_ref[...])


def _gmlp_mixer(x2d, g, w_in, b_in, gv, w_s, b_s_t, w_out):
    rows = x2d.shape[0]
    row_spec = pl.BlockSpec((ROW_TILE, D_MODEL), lambda i: (i, 0))
    return pl.pallas_call(
        _gmlp_kernel,
        grid=(rows // ROW_TILE,),
        in_specs=[
            row_spec,
            _resident((1, D_MODEL)),
            _resident((D_MODEL, 2 * GM_WIDTH)),
            _resident((1, 2 * GM_WIDTH)),
            _resident((1, GM_WIDTH)),
            _resident((GM_GROUPS, GM_CHUNK, GM_CHUNK)),
            _resident((GM_CHUNK, GM_GROUPS)),
            _resident((GM_WIDTH, D_MODEL)),
        ],
        out_specs=row_spec,
        out_shape=jax.ShapeDtypeStruct(x2d.shape, F32),
        scratch_shapes=[
            pltpu.VMEM((ROW_TILE, D_MODEL), BF16),
            pltpu.VMEM((ROW_TILE, GM_WIDTH), BF16),
            pltpu.VMEM((ROW_TILE, GM_WIDTH), BF16),
        ],
        compiler_params=pltpu.CompilerParams(
            dimension_semantics=("arbitrary",), vmem_limit_bytes=V7X_VMEM_LIMIT_BYTES),
        name="gmlp_mixer",
    )(x2d, g, w_in, b_in, gv, w_s, b_s_t, w_out)


def _rotary(t, cos, sin):
    half = RET_QK_DIM // 2
    t1, t2 = t[:, :half], t[:, half:]
    return jnp.concatenate([t1 * cos - t2 * sin, t2 * cos + t1 * sin], axis=-1)


def _ret_kernel(cdec_ref, x_ref, g_ref, win_ref, cos_ref, sin_ref, idec_ref, qdec_ref, kdec_ref,
                gh_ref, wout_ref, o_ref, h_ref, state_ref, og_ref):
    @pl.when(pl.program_id(1) == 0)
    def _():
        state_ref[...] = jnp.zeros_like(state_ref)

    x = x_ref[0]
    h_ref[...] = _rmsnorm(x, g_ref[...]).astype(BF16)
    cos = cos_ref[...]
    sin = sin_ref[...]
    k_base = RET_HEADS * RET_QK_DIM
    v_base = 2 * RET_HEADS * RET_QK_DIM
    g_base = v_base + RET_HEADS * RET_V_DIM
    for hd in range(RET_HEADS):
        qk_cols = slice(hd * RET_QK_DIM, (hd + 1) * RET_QK_DIM)
        v_cols = slice(hd * RET_V_DIM, (hd + 1) * RET_V_DIM)
        h = h_ref[...]
        q = _rotary(_dot(h, win_ref[:, qk_cols]), cos, sin)
        k = _rotary(_dot(h, win_ref[:, k_base + hd * RET_QK_DIM:k_base + (hd + 1) * RET_QK_DIM])
                    * (RET_QK_DIM ** -0.5), cos, sin)
        v = _dot(h, win_ref[:, v_base + hd * RET_V_DIM:v_base + (hd + 1) * RET_V_DIM]).astype(BF16)
        gate = _dot(h, win_ref[:, g_base + hd * RET_V_DIM:g_base + (hd + 1) * RET_V_DIM])
        outs = []
        for c in range(ROW_TILE // RET_CHUNK):
            rows = slice(c * RET_CHUNK, (c + 1) * RET_CHUNK)
            qc, kc, vc = q[rows], k[rows], v[rows]
            scores = lax.dot_general(qc.astype(BF16), kc.astype(BF16), (((1,), (1,)), ((), ())),
                                     preferred_element_type=F32) * idec_ref[hd]
            state = state_ref[hd]
            o = _dot(scores.astype(BF16), vc) + _dot((qc * qdec_ref[hd]).astype(BF16),
                                                    state.astype(BF16))
            kv = lax.dot_general((kc * kdec_ref[hd]).astype(BF16), vc, (((0,), (0,)), ((), ())),
                                 preferred_element_type=F32)
            state_ref[hd] = state * cdec_ref[hd] + kv
            outs.append(o)
        o = jnp.concatenate(outs, axis=0)
        o = _rmsnorm(o, gh_ref[:, v_cols])
        og_ref[:, v_cols] = (o * (gate / (1.0 + jnp.exp(-gate)))).astype(BF16)
    o_ref[0] = x + _dot(og_ref[...], wout_ref[...])


def _retention_tables(seq):
    heads, dk, chunk = RET_HEADS, RET_QK_DIM, RET_CHUNK
    pos = jnp.arange(seq, dtype=F32)
    inv_freq = 1.0 / (ROPE_BASE ** jnp.linspace(0.0, 1.0, dk // 2, dtype=F32))
    ang = pos[:, None] * inv_freq[None, :]
    log_gamma = jnp.log(1.0 - 2.0 ** (-5.0 - jnp.arange(heads, dtype=F32)))
    idx = jnp.arange(chunk, dtype=F32)
    diff = idx[:, None] - idx[None, :]
    inner = jnp.where(diff >= 0, jnp.exp(log_gamma[:, None, None] * jnp.maximum(diff, 0.0)), 0.0)
    qdec = jnp.exp(log_gamma[:, None] * (idx + 1.0))
    kdec = jnp.exp(log_gamma[:, None] * (chunk - 1.0 - idx))
    cdec = jnp.exp(log_gamma * chunk)
    qdec = jnp.broadcast_to(qdec[:, :, None], (heads, chunk, dk))
    kdec = jnp.broadcast_to(kdec[:, :, None], (heads, chunk, dk))
    return jnp.cos(ang), jnp.sin(ang), inner, qdec, kdec, cdec


def _retention_mixer(x, g, w_in, gh, w_out):
    batch, seq, _ = x.shape
    cos, sin, inner, qdec, kdec, cdec = _retention_tables(seq)
    proj = 2 * RET_HEADS * (RET_QK_DIM + RET_V_DIM)
    x_spec = pl.BlockSpec((1, ROW_TILE, D_MODEL), lambda b, j: (b, j, 0))
    rope_spec = pl.BlockSpec((ROW_TILE, RET_QK_DIM // 2), lambda b, j: (j, 0))
    return pl.pallas_call(
        _ret_kernel,
        grid=(batch, seq // ROW_TILE),
        in_specs=[
            pl.BlockSpec(memory_space=pltpu.SMEM),
            x_spec,
            _resident((1, D_MODEL)),
            _resident((D_MODEL, proj)),
            rope_spec,
            rope_spec,
            _resident((RET_HEADS, RET_CHUNK, RET_CHUNK)),
            _resident((RET_HEADS, RET_CHUNK, RET_QK_DIM)),
            _resident((RET_HEADS, RET_CHUNK, RET_QK_DIM)),
            _resident((1, RET_HEADS * RET_V_DIM)),
            _resident((RET_HEADS * RET_V_DIM, D_MODEL)),
        ],
        out_specs=x_spec,
        out_shape=jax.ShapeDtypeStruct(x.shape, F32),
        scratch_shapes=[
            pltpu.VMEM((ROW_TILE, D_MODEL), BF16),
            pltpu.VMEM((RET_HEADS, RET_QK_DIM, RET_V_DIM), F32),
            pltpu.VMEM((ROW_TILE, RET_HEADS * RET_V_DIM), BF16),
        ],
        compiler_params=pltpu.CompilerParams(
            dimension_semantics=("arbitrary", "arbitrary"),
            vmem_limit_bytes=V7X_VMEM_LIMIT_BYTES),
        name="retention_mixer",
    )(cdec, x, g, w_in, cos, sin, inner, qdec, kdec, gh, w_out)


def kernel(x, norm_mix_g, norm_ffn_g, a_w_in, a_b_in, a_v_norm_g, a_w_s, a_b_s, a_w_out,
           b_w_in, b_head_norm_g, b_w_out, mlp_w1, mlp_w2, final_norm_g):
    batch, seq, d = x.shape
    assert d == D_MODEL and seq % ROW_TILE == 0 and norm_mix_g.shape[0] == DEPTH == 2
    row = lambda p: p.reshape(1, -1)
    gf = row(final_norm_g)

    x2d = x.reshape(batch * seq, d)
    x2d = _gmlp_mixer(x2d, row(norm_mix_g[0]), a_w_in[0].astype(BF16), row(a_b_in[0]),
                      row(a_v_norm_g[0]), a_w_s[0], a_b_s[0].T, a_w_out[0].astype(BF16))
    x2d = _ffn(x2d, row(norm_ffn_g[0]), mlp_w1[0].astype(BF16), mlp_w2[0].astype(BF16), gf,
               final_norm=False)
    x3d = _retention_mixer(x2d.reshape(batch, seq, d), row(norm_mix_g[1]),
                           b_w_in[0].astype(BF16), row(b_head_norm_g[0]),
                           b_w_out[0].astype(BF16))
    x2d = _ffn(x3d.reshape(batch * seq, d), row(norm_ffn_g[1]), mlp_w1[1].astype(BF16),
               mlp_w2[1].astype(BF16), gf, final_norm=True)
    return x2d.reshape(batch, seq, d)
```

```python
import functools
import math

import jax
import jax.numpy as jnp
from jax import lax
from jax.experimental import pallas as pl
from jax.experimental.pallas import tpu as pltpu

D_MODEL = 1024
DEPTH = 2
RMS_EPS = 1e-6
GM_CHUNK = 128
GM_WIDTH = 2 * D_MODEL
GM_GROUPS = 8
GM_GROUP_DIM = GM_WIDTH // GM_GROUPS
RET_QK_DIM = 256
RET_HEADS = D_MODEL // RET_QK_DIM
RET_V_DIM = 2 * D_MODEL // RET_HEADS
ROPE_BASE = 10000.0
D_FF = 4 * D_MODEL

F32 = jnp.float32
BF16 = jnp.bfloat16

V7X_VMEM_LIMIT_BYTES = 56 * 1024 * 1024

MIXER_ROW_TILE = 512
FFN_ROW_TILE = 1024
FF_CHUNK = 1024
RET_CHUNK = 256


def _rmsnorm(x, g):
    ms = jnp.mean(x * x, axis=-1, keepdims=True)
    return x * lax.rsqrt(ms + RMS_EPS) * g


def _gelu_tanh(x):
    c = -2.0 * (2.0 / math.pi) ** 0.5 * math.log2(math.e)
    return x / (1.0 + jnp.exp2(x * (c + (0.044715 * c) * (x * x))))


def _dot(a, b):
    return jnp.dot(a, b, preferred_element_type=F32)


def _dot_nt(a, b):
    return lax.dot_general(a, b, (((1,), (1,)), ((), ())), preferred_element_type=F32)


def _dot_tn(a, b):
    return lax.dot_general(a, b, (((0,), (0,)), ((), ())), preferred_element_type=F32)


def _resident(shape):
    zeros = (0,) * len(shape)
    return pl.BlockSpec(shape, lambda *_: zeros, pipeline_mode=pl.Buffered(1))


def _compiler_params(n_grid_axes):
    return pltpu.CompilerParams(dimension_semantics=("arbitrary",) * n_grid_axes,
                                vmem_limit_bytes=V7X_VMEM_LIMIT_BYTES)


def _ffn_kernel(x_ref, g_ref, w1_ref, w2_ref, gf_ref, o_ref, *, final_norm):
    x = x_ref[...]
    h = _rmsnorm(x, g_ref[...]).astype(BF16)
    acc = x
    for c in range(D_FF // FF_CHUNK):
        cols = slice(c * FF_CHUNK, (c + 1) * FF_CHUNK)
        a = jnp.maximum(_dot(h, w1_ref[:, cols]), 0.0)
        acc = acc + _dot((a * a).astype(BF16), w2_ref[cols, :])
    if final_norm:
        acc = _rmsnorm(acc, gf_ref[...])
    o_ref[...] = acc


def _ffn(x2d, g, w1, w2, gf, *, final_norm):
    rows = x2d.shape[0]
    row_spec = pl.BlockSpec((FFN_ROW_TILE, D_MODEL), lambda i: (i, 0))
    return pl.pallas_call(
        functools.partial(_ffn_kernel, final_norm=final_norm),
        grid=(rows // FFN_ROW_TILE,),
        in_specs=[
            row_spec,
            _resident((1, D_MODEL)),
            _resident((D_MODEL, D_FF)),
            _resident((D_FF, D_MODEL)),
            _resident((1, D_MODEL)),
        ],
        out_specs=row_spec,
        out_shape=jax.ShapeDtypeStruct(x2d.shape, F32),
        compiler_params=_compiler_params(1),
        name="ffn_final" if final_norm else "ffn",
    )(x2d, g, w1, w2, gf)


def _gmlp_kernel(x_ref, g_ref, win_ref, bin_ref, gv_ref, ws_ref, bst_ref, wout_ref, o_ref,
                 h_ref, v_ref):
    x = x_ref[...]
    h_ref[...] = _rmsnorm(x, g_ref[...]).astype(BF16)

    def group_cols(grp):
        return slice(grp * GM_GROUP_DIM, (grp + 1) * GM_GROUP_DIM)

    def project(col0):
        cols = slice(col0, col0 + GM_GROUP_DIM)
        return _dot(h_ref[...], win_ref[:, cols]) + bin_ref[:, cols]

    v = _gelu_tanh(_dot(h_ref[...], win_ref[:, GM_WIDTH:]) + bin_ref[:, GM_WIDTH:])
    v_ref[...] = v
    v_scale = lax.rsqrt(jnp.mean(v * v, axis=-1, keepdims=True) + RMS_EPS)

    t_idx = lax.broadcasted_iota(jnp.int32, (GM_CHUNK, GM_CHUNK), 0)
    s_idx = lax.broadcasted_iota(jnp.int32, (GM_CHUNK, GM_CHUNK), 1)
    causal = s_idx <= t_idx
    acc = x
    u_pre_next = project(0)
    for grp in range(GM_GROUPS):
        cols = group_cols(grp)
        u_pre = u_pre_next
        if grp + 1 < GM_GROUPS:
            u_pre_next = project((grp + 1) * GM_GROUP_DIM)
        vn = (v_ref[:, cols] * v_scale * gv_ref[:, cols]).astype(BF16)
        w_causal = jnp.where(causal, ws_ref[grp], 0.0).astype(BF16)
        bias = bst_ref[:, grp:grp + 1]
        s = jnp.concatenate(
            [_dot(w_causal, vn[c * GM_CHUNK:(c + 1) * GM_CHUNK]) + bias
             for c in range(MIXER_ROW_TILE // GM_CHUNK)], axis=0)
        us = (_gelu_tanh(u_pre) * s).astype(BF16)
        acc = acc + _dot(us, wout_ref[cols, :])
    o_ref[...] = acc


def _gmlp_mixer(x2d, g, w_in, b_in, gv, w_s, b_s_t, w_out):
    rows = x2d.shape[0]
    row_spec = pl.BlockSpec((MIXER_ROW_TILE, D_MODEL), lambda i: (i, 0))
    return pl.pallas_call(
        _gmlp_kernel,
        grid=(rows // MIXER_ROW_TILE,),
        in_specs=[
            row_spec,
            _resident((1, D_MODEL)),
            _resident((D_MODEL, 2 * GM_WIDTH)),
            _resident((1, 2 * GM_WIDTH)),
            _resident((1, GM_WIDTH)),
            _resident((GM_GROUPS, GM_CHUNK, GM_CHUNK)),
            _resident((GM_CHUNK, GM_GROUPS)),
            _resident((GM_WIDTH, D_MODEL)),
        ],
        out_specs=row_spec,
        out_shape=jax.ShapeDtypeStruct(x2d.shape, F32),
        scratch_shapes=[
            pltpu.VMEM((MIXER_ROW_TILE, D_MODEL), BF16),
            pltpu.VMEM((MIXER_ROW_TILE, GM_WIDTH), F32),
        ],
        compiler_params=_compiler_params(1),
        name="gmlp_mixer",
    )(x2d, g, w_in, b_in, gv, w_s, b_s_t, w_out)


def _rotary(t, cos, sin):
    half = RET_QK_DIM // 2
    t1, t2 = t[:, :half], t[:, half:]
    return jnp.concatenate([t1 * cos - t2 * sin, t2 * cos + t1 * sin], axis=-1)


def _ret_kernel(cdec_ref, x_ref, g_ref, win_ref, cos_ref, sin_ref, idec_ref, qdec_ref, kdec_ref,
                gh_ref, wout_ref, o_ref,
                h_ref, state_ref, q_ref, qd_ref, k_ref, kd_ref, v_ref, og_ref):
    @pl.when(pl.program_id(1) == 0)
    def _():
        state_ref[...] = jnp.zeros_like(state_ref)

    n_chunks = MIXER_ROW_TILE // RET_CHUNK
    k_base = RET_HEADS * RET_QK_DIM
    v_base = 2 * RET_HEADS * RET_QK_DIM
    g_base = v_base + RET_HEADS * RET_V_DIM

    def chunk_rows(c):
        return slice(c * RET_CHUNK, (c + 1) * RET_CHUNK)

    def qk_cols(hd):
        return slice(hd * RET_QK_DIM, (hd + 1) * RET_QK_DIM)

    def v_cols(hd):
        return slice(hd * RET_V_DIM, (hd + 1) * RET_V_DIM)

    def w_cols(base, hd, width):
        return slice(base + hd * width, base + (hd + 1) * width)

    h_ref[...] = _rmsnorm(x_ref[0], g_ref[...]).astype(BF16)
    cos = cos_ref[...]
    sin = sin_ref[...]

    for hd in range(RET_HEADS):
        h = h_ref[...]
        q = _rotary(_dot(h, win_ref[:, w_cols(0, hd, RET_QK_DIM)]), cos, sin)
        k = _rotary(_dot(h, win_ref[:, w_cols(k_base, hd, RET_QK_DIM)]) * (RET_QK_DIM ** -0.5),
                    cos, sin)
        q_ref[:, qk_cols(hd)] = q.astype(BF16)
        k_ref[:, qk_cols(hd)] = k.astype(BF16)
        for c in range(n_chunks):
            rows = chunk_rows(c)
            qd_ref[rows, qk_cols(hd)] = (q[rows] * qdec_ref[hd]).astype(BF16)
            kd_ref[rows, qk_cols(hd)] = (k[rows] * kdec_ref[hd]).astype(BF16)
        v_ref[:, v_cols(hd)] = _dot(h, win_ref[:, w_cols(v_base, hd, RET_V_DIM)]).astype(BF16)

    for c in range(n_chunks):
        rows = chunk_rows(c)
        for hd in range(RET_HEADS):
            gate = _dot(h_ref[rows, :], win_ref[:, w_cols(g_base, hd, RET_V_DIM)])
            vb = v_ref[rows, v_cols(hd)]
            scores = _dot_nt(q_ref[rows, qk_cols(hd)], k_ref[rows, qk_cols(hd)]) * idec_ref[hd]
            state = state_ref[hd]
            o = _dot(scores.astype(BF16), vb) + _dot(qd_ref[rows, qk_cols(hd)],
                                                    state.astype(BF16))
            state_ref[hd] = state * cdec_ref[hd] + _dot_tn(kd_ref[rows, qk_cols(hd)], vb)
            o = _rmsnorm(o, gh_ref[:, v_cols(hd)])
            og_ref[rows, v_cols(hd)] = (o * (gate / (1.0 + jnp.exp(-gate)))).astype(BF16)
        o_ref[0, rows, :] = x_ref[0, rows, :] + _dot(og_ref[rows, :], wout_ref[...])


def _retention_tables(seq):
    heads, dk, chunk = RET_HEADS, RET_QK_DIM, RET_CHUNK
    pos = jnp.arange(seq, dtype=F32)
    inv_freq = 1.0 / (ROPE_BASE ** jnp.linspace(0.0, 1.0, dk // 2, dtype=F32))
    ang = pos[:, None] * inv_freq[None, :]
    log_gamma = jnp.log(1.0 - 2.0 ** (-5.0 - jnp.arange(heads, dtype=F32)))
    idx = jnp.arange(chunk, dtype=F32)
    diff = idx[:, None] - idx[None, :]
    inner = jnp.where(diff >= 0, jnp.exp(log_gamma[:, None, None] * jnp.maximum(diff, 0.0)), 0.0)
    qdec = jnp.exp(log_gamma[:, None] * (idx + 1.0))
    kdec = jnp.exp(log_gamma[:, None] * (chunk - 1.0 - idx))
    cdec = jnp.exp(log_gamma * chunk)
    qdec = jnp.broadcast_to(qdec[:, :, None], (heads, chunk, dk))
    kdec = jnp.broadcast_to(kdec[:, :, None], (heads, chunk, dk))
    return jnp.cos(ang), jnp.sin(ang), inner, qdec, kdec, cdec


def _retention_mixer(x, g, w_in, gh, w_out):
    batch, seq, _ = x.shape
    cos, sin, inner, qdec, kdec, cdec = _retention_tables(seq)
    proj = 2 * RET_HEADS * (RET_QK_DIM + RET_V_DIM)
    x_spec = pl.BlockSpec((1, MIXER_ROW_TILE, D_MODEL), lambda b, j: (b, j, 0))
    rope_spec = pl.BlockSpec((MIXER_ROW_TILE, RET_QK_DIM // 2), lambda b, j: (j, 0))
    qk_scratch = pltpu.VMEM((MIXER_ROW_TILE, RET_HEADS * RET_QK_DIM), BF16)
    v_scratch = pltpu.VMEM((MIXER_ROW_TILE, RET_HEADS * RET_V_DIM), BF16)
    return pl.pallas_call(
        _ret_kernel,
        grid=(batch, seq // MIXER_ROW_TILE),
        in_specs=[
            pl.BlockSpec(memory_space=pltpu.SMEM),
            x_spec,
            _resident((1, D_MODEL)),
            _resident((D_MODEL, proj)),
            rope_spec,
            rope_spec,
            _resident((RET_HEADS, RET_CHUNK, RET_CHUNK)),
            _resident((RET_HEADS, RET_CHUNK, RET_QK_DIM)),
            _resident((RET_HEADS, RET_CHUNK, RET_QK_DIM)),
            _resident((1, RET_HEADS * RET_V_DIM)),
            _resident((RET_HEADS * RET_V_DIM, D_MODEL)),
        ],
        out_specs=x_spec,
        out_shape=jax.ShapeDtypeStruct(x.shape, F32),
        scratch_shapes=[
            pltpu.VMEM((MIXER_ROW_TILE, D_MODEL), BF16),
            pltpu.VMEM((RET_HEADS, RET_QK_DIM, RET_V_DIM), F32),
            qk_scratch, qk_scratch, qk_scratch, qk_scratch,
            v_scratch,
            v_scratch,
        ],
        compiler_params=_compiler_params(2),
        name="retention_mixer",
    )(cdec, x, g, w_in, cos, sin, inner, qdec, kdec, gh, w_out)


def kernel(x, norm_mix_g, norm_ffn_g, a_w_in, a_b_in, a_v_norm_g, a_w_s, a_b_s, a_w_out,
           b_w_in, b_head_norm_g, b_w_out, mlp_w1, mlp_w2, final_norm_g):
    batch, seq, d = x.shape
    assert d == D_MODEL and norm_mix_g.shape[0] == DEPTH == 2
    assert seq % MIXER_ROW_TILE == 0 and (batch * seq) % FFN_ROW_TILE == 0
    row = lambda p: p.reshape(1, -1)
    gf = row(final_norm_g)

    x2d = x.reshape(batch * seq, d)
    x2d = _gmlp_mixer(x2d, row(norm_mix_g[0]), a_w_in[0].astype(BF16), row(a_b_in[0]),
                      row(a_v_norm_g[0]), a_w_s[0], a_b_s[0].T, a_w_out[0].astype(BF16))
    x2d = _ffn(x2d, row(norm_ffn_g[0]), mlp_w1[0].astype(BF16), mlp_w2[0].astype(BF16), gf,
               final_norm=False)
    x3d = _retention_mixer(x2d.reshape(batch, seq, d), row(norm_mix_g[1]),
                           b_w_in[0].astype(BF16), row(b_head_norm_g[0]),
                           b_w_out[0].astype(BF16))
    x2d = _ffn(x3d.reshape(batch * seq, d), row(norm_ffn_g[1]), mlp_w1[1].astype(BF16),
               mlp_w2[1].astype(BF16), gf, final_norm=True)
    return x2d.reshape(batch, seq, d)
```

```python
import functools
import math

import jax
import jax.numpy as jnp
from jax import lax
from jax.experimental import pallas as pl
from jax.experimental.pallas import tpu as pltpu

D_MODEL = 1024
DEPTH = 2
RMS_EPS = 1e-6
GM_CHUNK = 128
GM_WIDTH = 2 * D_MODEL
GM_GROUPS = 8
GM_GROUP_DIM = GM_WIDTH // GM_GROUPS
RET_QK_DIM = 256
RET_HEADS = D_MODEL // RET_QK_DIM
RET_V_DIM = 2 * D_MODEL // RET_HEADS
ROPE_BASE = 10000.0
D_FF = 4 * D_MODEL

F32 = jnp.float32
BF16 = jnp.bfloat16

V7X_VMEM_LIMIT_BYTES = 56 * 1024 * 1024

GMLP_ROW_TILE = 1024
GMLP_SUB_ROWS = 256
RET_ROW_TILE = 512
FFN_ROW_TILE = 1024
FF_CHUNK = 1024
RET_CHUNK = 256
WEIGHT_CHUNK_ROWS = 64


def _rmsnorm(x, g):
    ms = jnp.mean(x * x, axis=-1, keepdims=True)
    return x * lax.rsqrt(ms + RMS_EPS) * g


def _gelu_tanh(x):
    c = -2.0 * (2.0 / math.pi) ** 0.5 * math.log2(math.e)
    return x / (1.0 + jnp.exp2(x * (c + (0.044715 * c) * (x * x))))


def _dot(a, b):
    return jnp.dot(a, b, preferred_element_type=F32)


def _dot_nt(a, b):
    return lax.dot_general(a, b, (((1,), (1,)), ((), ())), preferred_element_type=F32)


def _dot_tn(a, b):
    return lax.dot_general(a, b, (((0,), (0,)), ((), ())), preferred_element_type=F32)


def _resident(shape):
    zeros = (0,) * len(shape)
    return pl.BlockSpec(shape, lambda *_: zeros, pipeline_mode=pl.Buffered(1))


def _fill_bf16(w_hbm, w_vmem, stage_ref, sem):
    rows, cols = w_hbm.shape
    n_chunks = rows // WEIGHT_CHUNK_ROWS

    def chunk_copy(k, slot):
        return pltpu.make_async_copy(
            w_hbm.at[pl.ds(k * WEIGHT_CHUNK_ROWS, WEIGHT_CHUNK_ROWS), :],
            stage_ref.at[slot, :, pl.ds(0, cols)], sem.at[slot])

    chunk_copy(0, 0).start()

    def body(k, carry):
        slot = k % 2

        @pl.when(k + 1 < n_chunks)
        def _():
            chunk_copy(k + 1, 1 - slot).start()

        chunk_copy(k, slot).wait()
        row0 = pl.multiple_of(k * WEIGHT_CHUNK_ROWS, WEIGHT_CHUNK_ROWS)
        w_vmem[pl.ds(row0, WEIGHT_CHUNK_ROWS), :] = stage_ref[slot, :, pl.ds(0, cols)].astype(BF16)
        return carry

    lax.fori_loop(0, n_chunks, body, 0)


def _weight_scratch(*shapes):
    widest = max(cols for _, cols in shapes)
    return ([pltpu.VMEM(shape, BF16) for shape in shapes]
            + [pltpu.VMEM((2, WEIGHT_CHUNK_ROWS, widest), F32), pltpu.SemaphoreType.DMA((2,))])


_HBM = pl.BlockSpec(memory_space=pl.ANY)


def _compiler_params(n_grid_axes):
    return pltpu.CompilerParams(dimension_semantics=("arbitrary",) * n_grid_axes,
                                vmem_limit_bytes=V7X_VMEM_LIMIT_BYTES)


def _ffn_kernel(x_ref, g_ref, w1_hbm, w2_hbm, gf_ref, o_ref, w1_ref, w2_ref, stage_ref, sem,
                *, layer, final_norm):
    @pl.when(pl.program_id(0) == 0)
    def _():
        _fill_bf16(w1_hbm.at[layer], w1_ref, stage_ref, sem)
        _fill_bf16(w2_hbm.at[layer], w2_ref, stage_ref, sem)

    x = x_ref[...]
    h = _rmsnorm(x, g_ref[...]).astype(BF16)
    acc = x
    for c in range(D_FF // FF_CHUNK):
        cols = slice(c * FF_CHUNK, (c + 1) * FF_CHUNK)
        a = jnp.maximum(_dot(h, w1_ref[:, cols]), 0.0)
        acc = acc + _dot((a * a).astype(BF16), w2_ref[cols, :])
    if final_norm:
        acc = _rmsnorm(acc, gf_ref[...])
    o_ref[...] = acc


def _ffn(x2d, g, w1, w2, gf, *, layer, final_norm):
    rows = x2d.shape[0]
    row_spec = pl.BlockSpec((FFN_ROW_TILE, D_MODEL), lambda i: (i, 0))
    return pl.pallas_call(
        functools.partial(_ffn_kernel, layer=layer, final_norm=final_norm),
        grid=(rows // FFN_ROW_TILE,),
        in_specs=[
            row_spec,
            _resident((1, D_MODEL)),
            _HBM,
            _HBM,
            _resident((1, D_MODEL)),
        ],
        out_specs=row_spec,
        out_shape=jax.ShapeDtypeStruct(x2d.shape, F32),
        scratch_shapes=_weight_scratch((D_MODEL, D_FF), (D_FF, D_MODEL)),
        compiler_params=_compiler_params(1),
        name="ffn_final" if final_norm else "ffn",
    )(x2d, g, w1, w2, gf)


def _gmlp_kernel(x_ref, g_ref, win_hbm, bin_ref, gv_ref, ws_ref, bst_ref, wout_hbm, o_ref,
                 h_ref, v_ref, win_ref, wout_ref, stage_ref, sem):
    @pl.when(pl.program_id(0) == 0)
    def _():
        _fill_bf16(win_hbm.at[0], win_ref, stage_ref, sem)
        _fill_bf16(wout_hbm.at[0], wout_ref, stage_ref, sem)

    n_sub = GMLP_ROW_TILE // GMLP_SUB_ROWS
    t_idx = lax.broadcasted_iota(jnp.int32, (GM_CHUNK, GM_CHUNK), 0)
    s_idx = lax.broadcasted_iota(jnp.int32, (GM_CHUNK, GM_CHUNK), 1)
    causal = s_idx <= t_idx

    def sub_rows(sub):
        return slice(sub * GMLP_SUB_ROWS, (sub + 1) * GMLP_SUB_ROWS)

    def group_cols(grp):
        return slice(grp * GM_GROUP_DIM, (grp + 1) * GM_GROUP_DIM)

    def project(sub, col0):
        cols = slice(col0, col0 + GM_GROUP_DIM)
        return _dot(h_ref[sub_rows(sub), :], win_ref[:, cols]) + bin_ref[:, cols]

    def normalise(sub):
        rows = sub_rows(sub)
        h_ref[rows, :] = _rmsnorm(x_ref[rows, :], g_ref[...]).astype(BF16)

    def v_group(sub, grp, sumsq):
        v = _gelu_tanh(project(sub, GM_WIDTH + grp * GM_GROUP_DIM))
        v_ref[sub_rows(sub), group_cols(grp)] = v
        return sumsq + jnp.sum(v * v, axis=-1, keepdims=True)

    def gate_group(sub, grp, v_scale):
        cols = group_cols(grp)
        vn = (v_ref[sub_rows(sub), cols] * v_scale * gv_ref[:, cols]).astype(BF16)
        w_causal = jnp.where(causal, ws_ref[grp], 0.0).astype(BF16)
        bias = bst_ref[:, grp:grp + 1]
        return jnp.concatenate(
            [_dot(w_causal, vn[c * GM_CHUNK:(c + 1) * GM_CHUNK]) + bias
             for c in range(GMLP_SUB_ROWS // GM_CHUNK)], axis=0)

    def out_group(grp, u_pre, s, acc):
        us = (_gelu_tanh(u_pre) * s).astype(BF16)
        return acc + _dot(us, wout_ref[group_cols(grp), :])

    zeros = jnp.zeros((GMLP_SUB_ROWS, 1), F32)
    normalise(0)
    sumsq = zeros
    for grp in range(GM_GROUPS):
        sumsq = v_group(0, grp, sumsq)
    for sub in range(n_sub):
        v_scale = lax.rsqrt(sumsq * (1.0 / GM_WIDTH) + RMS_EPS)
        has_next = sub + 1 < n_sub
        if has_next:
            normalise(sub + 1)
        sumsq = zeros
        acc = x_ref[sub_rows(sub), :]
        u_pre, s = project(sub, 0), gate_group(sub, 0, v_scale)
        for grp in range(GM_GROUPS):
            if grp + 1 < GM_GROUPS:
                u_pre_next = project(sub, (grp + 1) * GM_GROUP_DIM)
                s_next = gate_group(sub, grp + 1, v_scale)
            if has_next:
                sumsq = v_group(sub + 1, grp, sumsq)
            acc = out_group(grp, u_pre, s, acc)
            u_pre, s = u_pre_next, s_next
        o_ref[sub_rows(sub), :] = acc


def _gmlp_mixer(x2d, g, w_in, b_in, gv, w_s, b_s_t, w_out):
    rows = x2d.shape[0]
    row_spec = pl.BlockSpec((GMLP_ROW_TILE, D_MODEL), lambda i: (i, 0))
    return pl.pallas_call(
        _gmlp_kernel,
        grid=(rows // GMLP_ROW_TILE,),
        in_specs=[
            row_spec,
            _resident((1, D_MODEL)),
            _HBM,
            _resident((1, 2 * GM_WIDTH)),
            _resident((1, GM_WIDTH)),
            _resident((GM_GROUPS, GM_CHUNK, GM_CHUNK)),
            _resident((GM_CHUNK, GM_GROUPS)),
            _HBM,
        ],
        out_specs=row_spec,
        out_shape=jax.ShapeDtypeStruct(x2d.shape, F32),
        scratch_shapes=[
            pltpu.VMEM((GMLP_ROW_TILE, D_MODEL), BF16),
            pltpu.VMEM((GMLP_ROW_TILE, GM_WIDTH), F32),
        ] + _weight_scratch((D_MODEL, 2 * GM_WIDTH), (GM_WIDTH, D_MODEL)),
        compiler_params=_compiler_params(1),
        name="gmlp_mixer",
    )(x2d, g, w_in, b_in, gv, w_s, b_s_t, w_out)


def _rotary(t, cos, sin):
    half = RET_QK_DIM // 2
    t1, t2 = t[:, :half], t[:, half:]
    return jnp.concatenate([t1 * cos - t2 * sin, t2 * cos + t1 * sin], axis=-1)


def _ret_kernel(cdec_ref, x_ref, g_ref, win_hbm, cos_ref, sin_ref, idec_ref, qdec_ref, kdec_ref,
                gh_ref, wout_hbm, o_ref,
                h_ref, state_ref, q_ref, qd_ref, k_ref, kd_ref, v_ref, og_ref,
                win_ref, wout_ref, stage_ref, sem):
    @pl.when((pl.program_id(0) == 0) & (pl.program_id(1) == 0))
    def _():
        _fill_bf16(win_hbm.at[0], win_ref, stage_ref, sem)
        _fill_bf16(wout_hbm.at[0], wout_ref, stage_ref, sem)

    @pl.when(pl.program_id(1) == 0)
    def _():
        state_ref[...] = jnp.zeros_like(state_ref)

    n_chunks = RET_ROW_TILE // RET_CHUNK
    k_base = RET_HEADS * RET_QK_DIM
    v_base = 2 * RET_HEADS * RET_QK_DIM
    g_base = v_base + RET_HEADS * RET_V_DIM

    def chunk_rows(c):
        return slice(c * RET_CHUNK, (c + 1) * RET_CHUNK)

    def qk_cols(hd):
        return slice(hd * RET_QK_DIM, (hd + 1) * RET_QK_DIM)

    def v_cols(hd):
        return slice(hd * RET_V_DIM, (hd + 1) * RET_V_DIM)

    def w_cols(base, hd, width):
        return slice(base + hd * width, base + (hd + 1) * width)

    h_ref[...] = _rmsnorm(x_ref[0], g_ref[...]).astype(BF16)
    cos = cos_ref[...]
    sin = sin_ref[...]

    for hd in range(RET_HEADS):
        h = h_ref[...]
        q = _rotary(_dot(h, win_ref[:, w_cols(0, hd, RET_QK_DIM)]), cos, sin)
        k = _rotary(_dot(h, win_ref[:, w_cols(k_base, hd, RET_QK_DIM)]) * (RET_QK_DIM ** -0.5),
                    cos, sin)
        q_ref[:, qk_cols(hd)] = q.astype(BF16)
        k_ref[:, qk_cols(hd)] = k.astype(BF16)
        for c in range(n_chunks):
            rows = chunk_rows(c)
            qd_ref[rows, qk_cols(hd)] = (q[rows] * qdec_ref[hd]).astype(BF16)
            kd_ref[rows, qk_cols(hd)] = (k[rows] * kdec_ref[hd]).astype(BF16)
        v_ref[:, v_cols(hd)] = _dot(h, win_ref[:, w_cols(v_base, hd, RET_V_DIM)]).astype(BF16)

    for c in range(n_chunks):
        rows = chunk_rows(c)
        for hd in range(RET_HEADS):
            gate = _dot(h_ref[rows, :], win_ref[:, w_cols(g_base, hd, RET_V_DIM)])
            vb = v_ref[rows, v_cols(hd)]
            scores = _dot_nt(q_ref[rows, qk_cols(hd)], k_ref[rows, qk_cols(hd)]) * idec_ref[hd]
            state = state_ref[hd]
            o = _dot(scores.astype(BF16), vb) + _dot(qd_ref[rows, qk_cols(hd)],
                                                    state.astype(BF16))
            state_ref[hd] = state * cdec_ref[hd] + _dot_tn(kd_ref[rows, qk_cols(hd)], vb)
            o = _rmsnorm(o, gh_ref[:, v_cols(hd)])
            og_ref[rows, v_cols(hd)] = (o * (gate / (1.0 + jnp.exp(-gate)))).astype(BF16)
        o_ref[0, rows, :] = x_ref[0, rows, :] + _dot(og_ref[rows, :], wout_ref[...])


def _retention_tables(seq):
    heads, dk, chunk = RET_HEADS, RET_QK_DIM, RET_CHUNK
    pos = jnp.arange(seq, dtype=F32)
    inv_freq = 1.0 / (ROPE_BASE ** jnp.linspace(0.0, 1.0, dk // 2, dtype=F32))
    ang = pos[:, None] * inv_freq[None, :]
    log_gamma = jnp.log(1.0 - 2.0 ** (-5.0 - jnp.arange(heads, dtype=F32)))
    idx = jnp.arange(chunk, dtype=F32)
    diff = idx[:, None] - idx[None, :]
    inner = jnp.where(diff >= 0, jnp.exp(log_gamma[:, None, None] * jnp.maximum(diff, 0.0)), 0.0)
    qdec = jnp.exp(log_gamma[:, None] * (idx + 1.0))
    kdec = jnp.exp(log_gamma[:, None] * (chunk - 1.0 - idx))
    cdec = jnp.exp(log_gamma * chunk)
    qdec = jnp.broadcast_to(qdec[:, :, None], (heads, chunk, dk))
    kdec = jnp.broadcast_to(kdec[:, :, None], (heads, chunk, dk))
    return jnp.cos(ang), jnp.sin(ang), inner, qdec, kdec, cdec


def _retention_mixer(x, g, w_in, gh, w_out):
    batch, seq, _ = x.shape
    cos, sin, inner, qdec, kdec, cdec = _retention_tables(seq)
    proj = 2 * RET_HEADS * (RET_QK_DIM + RET_V_DIM)
    x_spec = pl.BlockSpec((1, RET_ROW_TILE, D_MODEL), lambda b, j: (b, j, 0))
    rope_spec = pl.BlockSpec((RET_ROW_TILE, RET_QK_DIM // 2), lambda b, j: (j, 0))
    qk_scratch = pltpu.VMEM((RET_ROW_TILE, RET_HEADS * RET_QK_DIM), BF16)
    v_scratch = pltpu.VMEM((RET_ROW_TILE, RET_HEADS * RET_V_DIM), BF16)
    return pl.pallas_call(
        _ret_kernel,
        grid=(batch, seq // RET_ROW_TILE),
        in_specs=[
            pl.BlockSpec(memory_space=pltpu.SMEM),
            x_spec,
            _resident((1, D_MODEL)),
            _HBM,
            rope_spec,
            rope_spec,
            _resident((RET_HEADS, RET_CHUNK, RET_CHUNK)),
            _resident((RET_HEADS, RET_CHUNK, RET_QK_DIM)),
            _resident((RET_HEADS, RET_CHUNK, RET_QK_DIM)),
            _resident((1, RET_HEADS * RET_V_DIM)),
            _HBM,
        ],
        out_specs=x_spec,
        out_shape=jax.ShapeDtypeStruct(x.shape, F32),
        scratch_shapes=[
            pltpu.VMEM((RET_ROW_TILE, D_MODEL), BF16),
            pltpu.VMEM((RET_HEADS, RET_QK_DIM, RET_V_DIM), F32),
            qk_scratch, qk_scratch, qk_scratch, qk_scratch,
            v_scratch,
            v_scratch,
        ] + _weight_scratch((D_MODEL, proj), (RET_HEADS * RET_V_DIM, D_MODEL)),
        compiler_params=_compiler_params(2),
        name="retention_mixer",
    )(cdec, x, g, w_in, cos, sin, inner, qdec, kdec, gh, w_out)


def kernel(x, norm_mix_g, norm_ffn_g, a_w_in, a_b_in, a_v_norm_g, a_w_s, a_b_s, a_w_out,
           b_w_in, b_head_norm_g, b_w_out, mlp_w1, mlp_w2, final_norm_g):
    batch, seq, d = x.shape
    assert d == D_MODEL and norm_mix_g.shape[0] == DEPTH == 2
    assert seq % RET_ROW_TILE == 0
    assert (batch * seq) % FFN_ROW_TILE == 0 and (batch * seq) % GMLP_ROW_TILE == 0
    row = lambda p: p.reshape(1, -1)
    gf = row(final_norm_g)

    x2d = x.reshape(batch * seq, d)
    x2d = _gmlp_mixer(x2d, row(norm_mix_g[0]), a_w_in, row(a_b_in[0]), row(a_v_norm_g[0]),
                      a_w_s[0], a_b_s[0].T, a_w_out)
    x2d = _ffn(x2d, row(norm_ffn_g[0]), mlp_w1, mlp_w2, gf, layer=0, final_norm=False)
    x3d = _retention_mixer(x2d.reshape(batch, seq, d), row(norm_mix_g[1]), b_w_in,
                           row(b_head_norm_g[0]), b_w_out)
    x2d = _ffn(x3d.reshape(batch * seq, d), row(norm_ffn_g[1]), mlp_w1, mlp_w2, gf, layer=1,
               final_norm=True)
    return x2d.reshape(batch, seq, d)
```

```python
import functools
import math

import jax
import jax.numpy as jnp
from jax import lax
from jax.experimental import pallas as pl
from jax.experimental.pallas import tpu as pltpu

D_MODEL = 1024
DEPTH = 2
RMS_EPS = 1e-6
GM_CHUNK = 128
GM_WIDTH = 2 * D_MODEL
GM_GROUPS = 8
GM_GROUP_DIM = GM_WIDTH // GM_GROUPS
RET_QK_DIM = 256
RET_HEADS = D_MODEL // RET_QK_DIM
RET_V_DIM = 2 * D_MODEL // RET_HEADS
ROPE_BASE = 10000.0
D_FF = 4 * D_MODEL

F32 = jnp.float32
BF16 = jnp.bfloat16

V7X_VMEM_LIMIT_BYTES = 56 * 1024 * 1024

GMLP_ROW_TILE = 1024
GMLP_SUB_ROWS = 256
RET_ROW_TILE = 512
FFN_ROW_TILE = 1024
FF_CHUNK = 1024
RET_CHUNK = 256
WEIGHT_CHUNK_BYTES = 2 * 1024 * 1024


def _rmsnorm(x, g):
    ms = jnp.mean(x * x, axis=-1, keepdims=True)
    return x * lax.rsqrt(ms + RMS_EPS) * g


def _gelu_tanh(x):
    c = -2.0 * (2.0 / math.pi) ** 0.5 * math.log2(math.e)
    return x / (1.0 + jnp.exp2(x * (c + (0.044715 * c) * (x * x))))


def _dot(a, b):
    return jnp.dot(a, b, preferred_element_type=F32)


def _dot_nt(a, b):
    return lax.dot_general(a, b, (((1,), (1,)), ((), ())), preferred_element_type=F32)


def _dot_tn(a, b):
    return lax.dot_general(a, b, (((0,), (0,)), ((), ())), preferred_element_type=F32)


def _resident(shape):
    zeros = (0,) * len(shape)
    return pl.BlockSpec(shape, lambda *_: zeros, pipeline_mode=pl.Buffered(1))


def _fill_bf16(w_hbm, w_vmem):
    rows, cols = w_hbm.shape
    chunk_rows = 1 << ((WEIGHT_CHUNK_BYTES // (cols * 4)).bit_length() - 1)
    n_chunks = rows // chunk_rows
    assert n_chunks * chunk_rows == rows and chunk_rows % 16 == 0

    def fill(stage_ref, sem):
        def chunk_copy(k, slot):
            return pltpu.make_async_copy(w_hbm.at[pl.ds(k * chunk_rows, chunk_rows), :],
                                         stage_ref.at[slot], sem.at[slot])

        chunk_copy(0, 0).start()

        def body(k, carry):
            slot = k % 2

            @pl.when(k + 1 < n_chunks)
            def _():
                chunk_copy(k + 1, 1 - slot).start()

            chunk_copy(k, slot).wait()
            row0 = pl.multiple_of(k * chunk_rows, chunk_rows)
            w_vmem[pl.ds(row0, chunk_rows), :] = stage_ref[slot].astype(BF16)
            return carry

        lax.fori_loop(0, n_chunks, body, 0)

    pl.run_scoped(fill, pltpu.VMEM((2, chunk_rows, cols), F32), pltpu.SemaphoreType.DMA((2,)))


def _weight_scratch(*shapes):
    return [pltpu.VMEM(shape, BF16) for shape in shapes]


_HBM = pl.BlockSpec(memory_space=pl.ANY)


def _compiler_params(n_grid_axes):
    return pltpu.CompilerParams(dimension_semantics=("arbitrary",) * n_grid_axes,
                                vmem_limit_bytes=V7X_VMEM_LIMIT_BYTES)


def _ffn_kernel(x_ref, g_ref, w1_hbm, w2_hbm, gf_ref, o_ref, w1_ref, w2_ref, *, layer, final_norm):
    @pl.when(pl.program_id(0) == 0)
    def _():
        _fill_bf16(w1_hbm.at[layer], w1_ref)
        _fill_bf16(w2_hbm.at[layer], w2_ref)

    x = x_ref[...]
    h = _rmsnorm(x, g_ref[...]).astype(BF16)
    acc = x
    for c in range(D_FF // FF_CHUNK):
        cols = slice(c * FF_CHUNK, (c + 1) * FF_CHUNK)
        a = jnp.maximum(_dot(h, w1_ref[:, cols]), 0.0)
        acc = acc + _dot((a * a).astype(BF16), w2_ref[cols, :])
    if final_norm:
        acc = _rmsnorm(acc, gf_ref[...])
    o_ref[...] = acc


def _ffn(x2d, g, w1, w2, gf, *, layer, final_norm):
    rows = x2d.shape[0]
    row_spec = pl.BlockSpec((FFN_ROW_TILE, D_MODEL), lambda i: (i, 0))
    return pl.pallas_call(
        functools.partial(_ffn_kernel, layer=layer, final_norm=final_norm),
        grid=(rows // FFN_ROW_TILE,),
        in_specs=[
            row_spec,
            _resident((1, D_MODEL)),
            _HBM,
            _HBM,
            _resident((1, D_MODEL)),
        ],
        out_specs=row_spec,
        out_shape=jax.ShapeDtypeStruct(x2d.shape, F32),
        scratch_shapes=_weight_scratch((D_MODEL, D_FF), (D_FF, D_MODEL)),
        compiler_params=_compiler_params(1),
        name="ffn_final" if final_norm else "ffn",
    )(x2d, g, w1, w2, gf)


def _gmlp_kernel(x_ref, g_ref, win_hbm, bin_ref, gv_ref, ws_ref, bst_ref, wout_hbm, o_ref,
                 h_ref, v_ref, win_ref, wout_ref):
    @pl.when(pl.program_id(0) == 0)
    def _():
        _fill_bf16(win_hbm.at[0], win_ref)
        _fill_bf16(wout_hbm.at[0], wout_ref)

    n_sub = GMLP_ROW_TILE // GMLP_SUB_ROWS
    t_idx = lax.broadcasted_iota(jnp.int32, (GM_CHUNK, GM_CHUNK), 0)
    s_idx = lax.broadcasted_iota(jnp.int32, (GM_CHUNK, GM_CHUNK), 1)
    causal = s_idx <= t_idx

    def sub_rows(sub):
        return slice(sub * GMLP_SUB_ROWS, (sub + 1) * GMLP_SUB_ROWS)

    def group_cols(grp):
        return slice(grp * GM_GROUP_DIM, (grp + 1) * GM_GROUP_DIM)

    def project(sub, col0):
        cols = slice(col0, col0 + GM_GROUP_DIM)
        return _dot(h_ref[sub_rows(sub), :], win_ref[:, cols]) + bin_ref[:, cols]

    def normalise(sub):
        rows = sub_rows(sub)
        h_ref[rows, :] = _rmsnorm(x_ref[rows, :], g_ref[...]).astype(BF16)

    def v_group(sub, grp, sumsq):
        v = _gelu_tanh(project(sub, GM_WIDTH + grp * GM_GROUP_DIM))
        v_ref[sub_rows(sub), group_cols(grp)] = v
        return sumsq + jnp.sum(v * v, axis=-1, keepdims=True)

    def gate_group(sub, grp, v_scale):
        cols = group_cols(grp)
        vn = (v_ref[sub_rows(sub), cols] * v_scale * gv_ref[:, cols]).astype(BF16)
        w_causal = jnp.where(causal, ws_ref[grp], 0.0).astype(BF16)
        bias = bst_ref[:, grp:grp + 1]
        return jnp.concatenate(
            [_dot(w_causal, vn[c * GM_CHUNK:(c + 1) * GM_CHUNK]) + bias
             for c in range(GMLP_SUB_ROWS // GM_CHUNK)], axis=0)

    def out_group(grp, u_pre, s, acc):
        us = (_gelu_tanh(u_pre) * s).astype(BF16)
        return acc + _dot(us, wout_ref[group_cols(grp), :])

    zeros = jnp.zeros((GMLP_SUB_ROWS, 1), F32)
    normalise(0)
    sumsq = zeros
    for grp in range(GM_GROUPS):
        sumsq = v_group(0, grp, sumsq)
    for sub in range(n_sub):
        v_scale = lax.rsqrt(sumsq * (1.0 / GM_WIDTH) + RMS_EPS)
        has_next = sub + 1 < n_sub
        if has_next:
            normalise(sub + 1)
        sumsq = zeros
        acc = x_ref[sub_rows(sub), :]
        u_pre, s = project(sub, 0), gate_group(sub, 0, v_scale)
        for grp in range(GM_GROUPS):
            if grp + 1 < GM_GROUPS:
                u_pre_next = project(sub, (grp + 1) * GM_GROUP_DIM)
                s_next = gate_group(sub, grp + 1, v_scale)
            if has_next:
                sumsq = v_group(sub + 1, grp, sumsq)
            acc = out_group(grp, u_pre, s, acc)
            u_pre, s = u_pre_next, s_next
        o_ref[sub_rows(sub), :] = acc


def _gmlp_mixer(x2d, g, w_in, b_in, gv, w_s, b_s_t, w_out):
    rows = x2d.shape[0]
    row_spec = pl.BlockSpec((GMLP_ROW_TILE, D_MODEL), lambda i: (i, 0))
    return pl.pallas_call(
        _gmlp_kernel,
        grid=(rows // GMLP_ROW_TILE,),
        in_specs=[
            row_spec,
            _resident((1, D_MODEL)),
            _HBM,
            _resident((1, 2 * GM_WIDTH)),
            _resident((1, GM_WIDTH)),
            _resident((GM_GROUPS, GM_CHUNK, GM_CHUNK)),
            _resident((GM_CHUNK, GM_GROUPS)),
            _HBM,
        ],
        out_specs=row_spec,
        out_shape=jax.ShapeDtypeStruct(x2d.shape, F32),
        scratch_shapes=[
            pltpu.VMEM((GMLP_ROW_TILE, D_MODEL), BF16),
            pltpu.VMEM((GMLP_ROW_TILE, GM_WIDTH), F32),
        ] + _weight_scratch((D_MODEL, 2 * GM_WIDTH), (GM_WIDTH, D_MODEL)),
        compiler_params=_compiler_params(1),
        name="gmlp_mixer",
    )(x2d, g, w_in, b_in, gv, w_s, b_s_t, w_out)


def _rotary(t, cos, sin):
    half = RET_QK_DIM // 2
    t1, t2 = t[:, :half], t[:, half:]
    return jnp.concatenate([t1 * cos - t2 * sin, t2 * cos + t1 * sin], axis=-1)


def _ret_kernel(cdec_ref, x_ref, g_ref, win_hbm, cos_ref, sin_ref, idec_ref, qdec_ref, kdec_ref,
                gh_ref, wout_hbm, o_ref,
                h_ref, state_ref, q_ref, qd_ref, k_ref, kd_ref, v_ref, og_ref,
                win_ref, wout_ref):
    @pl.when((pl.program_id(0) == 0) & (pl.program_id(1) == 0))
    def _():
        _fill_bf16(win_hbm.at[0], win_ref)
        _fill_bf16(wout_hbm.at[0], wout_ref)

    @pl.when(pl.program_id(1) == 0)
    def _():
        state_ref[...] = jnp.zeros_like(state_ref)

    n_chunks = RET_ROW_TILE // RET_CHUNK
    k_base = RET_HEADS * RET_QK_DIM
    v_base = 2 * RET_HEADS * RET_QK_DIM
    g_base = v_base + RET_HEADS * RET_V_DIM

    def chunk_rows(c):
        return slice(c * RET_CHUNK, (c + 1) * RET_CHUNK)

    def qk_cols(hd):
        return slice(hd * RET_QK_DIM, (hd + 1) * RET_QK_DIM)

    def v_cols(hd):
        return slice(hd * RET_V_DIM, (hd + 1) * RET_V_DIM)

    def w_cols(base, hd, width):
        return slice(base + hd * width, base + (hd + 1) * width)

    h_ref[...] = _rmsnorm(x_ref[0], g_ref[...]).astype(BF16)
    cos = cos_ref[...]
    sin = sin_ref[...]

    for hd in range(RET_HEADS):
        h = h_ref[...]
        q = _rotary(_dot(h, win_ref[:, w_cols(0, hd, RET_QK_DIM)]), cos, sin)
        k = _rotary(_dot(h, win_ref[:, w_cols(k_base, hd, RET_QK_DIM)]) * (RET_QK_DIM ** -0.5),
                    cos, sin)
        q_ref[:, qk_cols(hd)] = q.astype(BF16)
        k_ref[:, qk_cols(hd)] = k.astype(BF16)
        for c in range(n_chunks):
            rows = chunk_rows(c)
            qd_ref[rows, qk_cols(hd)] = (q[rows] * qdec_ref[hd]).astype(BF16)
            kd_ref[rows, qk_cols(hd)] = (k[rows] * kdec_ref[hd]).astype(BF16)
        v_ref[:, v_cols(hd)] = _dot(h, win_ref[:, w_cols(v_base, hd, RET_V_DIM)]).astype(BF16)

    for c in range(n_chunks):
        rows = chunk_rows(c)
        for hd in range(RET_HEADS):
            gate = _dot(h_ref[rows, :], win_ref[:, w_cols(g_base, hd, RET_V_DIM)])
            vb = v_ref[rows, v_cols(hd)]
            scores = _dot_nt(q_ref[rows, qk_cols(hd)], k_ref[rows, qk_cols(hd)]) * idec_ref[hd]
            state = state_ref[hd]
            o = _dot(scores.astype(BF16), vb) + _dot(qd_ref[rows, qk_cols(hd)],
                                                    state.astype(BF16))
            state_ref[hd] = state * cdec_ref[hd] + _dot_tn(kd_ref[rows, qk_cols(hd)], vb)
            o = _rmsnorm(o, gh_ref[:, v_cols(hd)])
            og_ref[rows, v_cols(hd)] = (o * (gate / (1.0 + jnp.exp(-gate)))).astype(BF16)
        o_ref[0, rows, :] = x_ref[0, rows, :] + _dot(og_ref[rows, :], wout_ref[...])


def _retention_tables(seq):
    heads, dk, chunk = RET_HEADS, RET_QK_DIM, RET_CHUNK
    pos = jnp.arange(seq, dtype=F32)
    inv_freq = 1.0 / (ROPE_BASE ** jnp.linspace(0.0, 1.0, dk // 2, dtype=F32))
    ang = pos[:, None] * inv_freq[None, :]
    log_gamma = jnp.log(1.0 - 2.0 ** (-5.0 - jnp.arange(heads, dtype=F32)))
    idx = jnp.arange(chunk, dtype=F32)
    diff = idx[:, None] - idx[None, :]
    inner = jnp.where(diff >= 0, jnp.exp(log_gamma[:, None, None] * jnp.maximum(diff, 0.0)), 0.0)
    qdec = jnp.exp(log_gamma[:, None] * (idx + 1.0))
    kdec = jnp.exp(log_gamma[:, None] * (chunk - 1.0 - idx))
    cdec = jnp.exp(log_gamma * chunk)
    qdec = jnp.broadcast_to(qdec[:, :, None], (heads, chunk, dk))
    kdec = jnp.broadcast_to(kdec[:, :, None], (heads, chunk, dk))
    return jnp.cos(ang), jnp.sin(ang), inner, qdec, kdec, cdec


def _retention_mixer(x, g, w_in, gh, w_out):
    batch, seq, _ = x.shape
    cos, sin, inner, qdec, kdec, cdec = _retention_tables(seq)
    proj = 2 * RET_HEADS * (RET_QK_DIM + RET_V_DIM)
    x_spec = pl.BlockSpec((1, RET_ROW_TILE, D_MODEL), lambda b, j: (b, j, 0))
    rope_spec = pl.BlockSpec((RET_ROW_TILE, RET_QK_DIM // 2), lambda b, j: (j, 0))
    qk_scratch = pltpu.VMEM((RET_ROW_TILE, RET_HEADS * RET_QK_DIM), BF16)
    v_scratch = pltpu.VMEM((RET_ROW_TILE, RET_HEADS * RET_V_DIM), BF16)
    return pl.pallas_call(
        _ret_kernel,
        grid=(batch, seq // RET_ROW_TILE),
        in_specs=[
            pl.BlockSpec(memory_space=pltpu.SMEM),
            x_spec,
            _resident((1, D_MODEL)),
            _HBM,
            rope_spec,
            rope_spec,
            _resident((RET_HEADS, RET_CHUNK, RET_CHUNK)),
            _resident((RET_HEADS, RET_CHUNK, RET_QK_DIM)),
            _resident((RET_HEADS, RET_CHUNK, RET_QK_DIM)),
            _resident((1, RET_HEADS * RET_V_DIM)),
            _HBM,
        ],
        out_specs=x_spec,
        out_shape=jax.ShapeDtypeStruct(x.shape, F32),
        scratch_shapes=[
            pltpu.VMEM((RET_ROW_TILE, D_MODEL), BF16),
            pltpu.VMEM((RET_HEADS, RET_QK_DIM, RET_V_DIM), F32),
            qk_scratch, qk_scratch, qk_scratch, qk_scratch,
            v_scratch,
            v_scratch,
        ] + _weight_scratch((D_MODEL, proj), (RET_HEADS * RET_V_DIM, D_MODEL)),
        compiler_params=_compiler_params(2),
        name="retention_mixer",
    )(cdec, x, g, w_in, cos, sin, inner, qdec, kdec, gh, w_out)


def kernel(x, norm_mix_g, norm_ffn_g, a_w_in, a_b_in, a_v_norm_g, a_w_s, a_b_s, a_w_out,
           b_w_in, b_head_norm_g, b_w_out, mlp_w1, mlp_w2, final_norm_g):
    batch, seq, d = x.shape
    assert d == D_MODEL and norm_mix_g.shape[0] == DEPTH == 2
    assert seq % RET_ROW_TILE == 0
    assert (batch * seq) % FFN_ROW_TILE == 0 and (batch * seq) % GMLP_ROW_TILE == 0
    row = lambda p: p.reshape(1, -1)
    gf = row(final_norm_g)

    x2d = x.reshape(batch * seq, d)
    x2d = _gmlp_mixer(x2d, row(norm_mix_g[0]), a_w_in, row(a_b_in[0]), row(a_v_norm_g[0]),
                      a_w_s[0], a_b_s[0].T, a_w_out)
    x2d = _ffn(x2d, row(norm_ffn_g[0]), mlp_w1, mlp_w2, gf, layer=0, final_norm=False)
    x3d = _retention_mixer(x2d.reshape(batch, seq, d), row(norm_mix_g[1]), b_w_in,
                           row(b_head_norm_g[0]), b_w_out)
    x2d = _ffn(x3d.reshape(batch * seq, d), row(norm_ffn_g[1]), mlp_w1, mlp_w2, gf, layer=1,
               final_norm=True)
    return x2d.reshape(batch, seq, d)
```

```python
import functools
import math

import jax
import jax.numpy as jnp
from jax import lax
from jax.experimental import pallas as pl
from jax.experimental.pallas import tpu as pltpu

D_MODEL = 1024
DEPTH = 2
RMS_EPS = 1e-6
GM_CHUNK = 128
GM_WIDTH = 2 * D_MODEL
GM_GROUPS = 8
GM_GROUP_DIM = GM_WIDTH // GM_GROUPS
RET_QK_DIM = 256
RET_HEADS = D_MODEL // RET_QK_DIM
RET_V_DIM = 2 * D_MODEL // RET_HEADS
ROPE_BASE = 10000.0
D_FF = 4 * D_MODEL

F32 = jnp.float32
BF16 = jnp.bfloat16

V7X_VMEM_LIMIT_BYTES = 56 * 1024 * 1024

GMLP_ROW_TILE = 1024
GMLP_SUB_ROWS = 256
RET_ROW_TILE = 512
FFN_ROW_TILE = 1024
FFN_NORM_ROWS = 256
FF_CHUNK = 1024
RET_CHUNK = 256
WEIGHT_CHUNK_BYTES = 2 * 1024 * 1024


def _rmsnorm(x, g):
    ms = jnp.mean(x * x, axis=-1, keepdims=True)
    return x * lax.rsqrt(ms + RMS_EPS) * g


def _gelu_tanh(x):
    c = -2.0 * (2.0 / math.pi) ** 0.5 * math.log2(math.e)
    return x / (1.0 + jnp.exp2(x * (c + (0.044715 * c) * (x * x))))


def _dot(a, b):
    return jnp.dot(a, b, preferred_element_type=F32)


def _dot_nt(a, b):
    return lax.dot_general(a, b, (((1,), (1,)), ((), ())), preferred_element_type=F32)


def _dot_tn(a, b):
    return lax.dot_general(a, b, (((0,), (0,)), ((), ())), preferred_element_type=F32)


def _resident(shape):
    zeros = (0,) * len(shape)
    return pl.BlockSpec(shape, lambda *_: zeros, pipeline_mode=pl.Buffered(1))


def _fill_bf16(w_hbm, w_vmem):
    rows, cols = w_hbm.shape
    chunk_rows = 1 << ((WEIGHT_CHUNK_BYTES // (cols * 4)).bit_length() - 1)
    n_chunks = rows // chunk_rows
    assert n_chunks * chunk_rows == rows and chunk_rows % 16 == 0

    def fill(stage_ref, sem):
        def chunk_copy(k, slot):
            return pltpu.make_async_copy(w_hbm.at[pl.ds(k * chunk_rows, chunk_rows), :],
                                         stage_ref.at[slot], sem.at[slot])

        chunk_copy(0, 0).start()

        def body(k, carry):
            slot = k % 2

            @pl.when(k + 1 < n_chunks)
            def _():
                chunk_copy(k + 1, 1 - slot).start()

            chunk_copy(k, slot).wait()
            row0 = pl.multiple_of(k * chunk_rows, chunk_rows)
            w_vmem[pl.ds(row0, chunk_rows), :] = stage_ref[slot].astype(BF16)
            return carry

        lax.fori_loop(0, n_chunks, body, 0)

    pl.run_scoped(fill, pltpu.VMEM((2, chunk_rows, cols), F32), pltpu.SemaphoreType.DMA((2,)))


def _weight_scratch(*shapes):
    return [pltpu.VMEM(shape, BF16) for shape in shapes]


_HBM = pl.BlockSpec(memory_space=pl.ANY)


def _compiler_params(n_grid_axes):
    return pltpu.CompilerParams(dimension_semantics=("arbitrary",) * n_grid_axes,
                                vmem_limit_bytes=V7X_VMEM_LIMIT_BYTES)


def _ffn_kernel(x_ref, g_ref, w1_hbm, w2_hbm, gf_ref, o_ref, w1_ref, w2_ref, *, layer, final_norm):
    @pl.when(pl.program_id(0) == 0)
    def _():
        _fill_bf16(w1_hbm.at[layer], w1_ref)
        _fill_bf16(w2_hbm.at[layer], w2_ref)

    x = x_ref[...]
    h_blocks = [_rmsnorm(x[r:r + FFN_NORM_ROWS], g_ref[...]).astype(BF16)
                for r in range(0, FFN_ROW_TILE, FFN_NORM_ROWS)]
    h = jnp.concatenate(h_blocks, axis=0)
    acc = x
    for c in range(D_FF // FF_CHUNK):
        cols = slice(c * FF_CHUNK, (c + 1) * FF_CHUNK)
        if c == 0:
            a = jnp.concatenate([_dot(hb, w1_ref[:, cols]) for hb in h_blocks], axis=0)
        else:
            a = _dot(h, w1_ref[:, cols])
        a = jnp.maximum(a, 0.0)
        a = (a * a).astype(BF16)
        if c + 1 < D_FF // FF_CHUNK:
            acc = acc + _dot(a, w2_ref[cols, :])
    for r in range(0, FFN_ROW_TILE, FFN_NORM_ROWS):
        out = acc[r:r + FFN_NORM_ROWS] + _dot(a[r:r + FFN_NORM_ROWS], w2_ref[cols, :])
        if final_norm:
            out = _rmsnorm(out, gf_ref[...])
        o_ref[r:r + FFN_NORM_ROWS, :] = out


def _ffn(x2d, g, w1, w2, gf, *, layer, final_norm):
    rows = x2d.shape[0]
    row_spec = pl.BlockSpec((FFN_ROW_TILE, D_MODEL), lambda i: (i, 0))
    return pl.pallas_call(
        functools.partial(_ffn_kernel, layer=layer, final_norm=final_norm),
        grid=(rows // FFN_ROW_TILE,),
        in_specs=[
            row_spec,
            _resident((1, D_MODEL)),
            _HBM,
            _HBM,
            _resident((1, D_MODEL)),
        ],
        out_specs=row_spec,
        out_shape=jax.ShapeDtypeStruct(x2d.shape, F32),
        scratch_shapes=_weight_scratch((D_MODEL, D_FF), (D_FF, D_MODEL)),
        compiler_params=_compiler_params(1),
        name="ffn_final" if final_norm else "ffn",
    )(x2d, g, w1, w2, gf)


def _gmlp_kernel(x_ref, g_ref, win_hbm, bin_ref, gv_ref, ws_ref, bst_ref, wout_hbm, o_ref,
                 h_ref, v_ref, win_ref, wout_ref):
    @pl.when(pl.program_id(0) == 0)
    def _():
        _fill_bf16(win_hbm.at[0], win_ref)
        _fill_bf16(wout_hbm.at[0], wout_ref)

    n_sub = GMLP_ROW_TILE // GMLP_SUB_ROWS
    t_idx = lax.broadcasted_iota(jnp.int32, (GM_CHUNK, GM_CHUNK), 0)
    s_idx = lax.broadcasted_iota(jnp.int32, (GM_CHUNK, GM_CHUNK), 1)
    causal = s_idx <= t_idx

    def sub_rows(sub):
        return slice(sub * GMLP_SUB_ROWS, (sub + 1) * GMLP_SUB_ROWS)

    def group_cols(grp):
        return slice(grp * GM_GROUP_DIM, (grp + 1) * GM_GROUP_DIM)

    def project(sub, col0):
        cols = slice(col0, col0 + GM_GROUP_DIM)
        return _dot(h_ref[sub_rows(sub), :], win_ref[:, cols]) + bin_ref[:, cols]

    def normalise(sub):
        rows = sub_rows(sub)
        h_ref[rows, :] = _rmsnorm(x_ref[rows, :], g_ref[...]).astype(BF16)

    def v_group(sub, grp, sumsq):
        v = _gelu_tanh(project(sub, GM_WIDTH + grp * GM_GROUP_DIM))
        v_ref[sub_rows(sub), group_cols(grp)] = v
        return sumsq + jnp.sum(v * v, axis=-1, keepdims=True)

    def gate_group(sub, grp, v_scale):
        cols = group_cols(grp)
        vn = (v_ref[sub_rows(sub), cols] * v_scale * gv_ref[:, cols]).astype(BF16)
        w_causal = jnp.where(causal, ws_ref[grp], 0.0).astype(BF16)
        bias = bst_ref[:, grp:grp + 1]
        return jnp.concatenate(
            [_dot(w_causal, vn[c * GM_CHUNK:(c + 1) * GM_CHUNK]) + bias
             for c in range(GMLP_SUB_ROWS // GM_CHUNK)], axis=0)

    def out_group(grp, u_pre, s, acc):
        us = (_gelu_tanh(u_pre) * s).astype(BF16)
        return acc + _dot(us, wout_ref[group_cols(grp), :])

    zeros = jnp.zeros((GMLP_SUB_ROWS, 1), F32)
    normalise(0)
    sumsq = zeros
    for grp in range(GM_GROUPS):
        sumsq = v_group(0, grp, sumsq)
    for sub in range(n_sub):
        v_scale = lax.rsqrt(sumsq * (1.0 / GM_WIDTH) + RMS_EPS)
        has_next = sub + 1 < n_sub
        if has_next:
            normalise(sub + 1)
        sumsq = zeros
        acc = x_ref[sub_rows(sub), :]
        u_pre, s = project(sub, 0), gate_group(sub, 0, v_scale)
        for grp in range(GM_GROUPS):
            if grp + 1 < GM_GROUPS:
                u_pre_next = project(sub, (grp + 1) * GM_GROUP_DIM)
                s_next = gate_group(sub, grp + 1, v_scale)
            if has_next:
                sumsq = v_group(sub + 1, grp, sumsq)
            acc = out_group(grp, u_pre, s, acc)
            u_pre, s = u_pre_next, s_next
        o_ref[sub_rows(sub), :] = acc


def _gmlp_mixer(x2d, g, w_in, b_in, gv, w_s, b_s_t, w_out):
    rows = x2d.shape[0]
    row_spec = pl.BlockSpec((GMLP_ROW_TILE, D_MODEL), lambda i: (i, 0))
    return pl.pallas_call(
        _gmlp_kernel,
        grid=(rows // GMLP_ROW_TILE,),
        in_specs=[
            row_spec,
            _resident((1, D_MODEL)),
            _HBM,
            _resident((1, 2 * GM_WIDTH)),
            _resident((1, GM_WIDTH)),
            _resident((GM_GROUPS, GM_CHUNK, GM_CHUNK)),
            _resident((GM_CHUNK, GM_GROUPS)),
            _HBM,
        ],
        out_specs=row_spec,
        out_shape=jax.ShapeDtypeStruct(x2d.shape, F32),
        scratch_shapes=[
            pltpu.VMEM((GMLP_ROW_TILE, D_MODEL), BF16),
            pltpu.VMEM((GMLP_ROW_TILE, GM_WIDTH), F32),
        ] + _weight_scratch((D_MODEL, 2 * GM_WIDTH), (GM_WIDTH, D_MODEL)),
        compiler_params=_compiler_params(1),
        name="gmlp_mixer",
    )(x2d, g, w_in, b_in, gv, w_s, b_s_t, w_out)


def _rotary(t, cos, sin):
    half = RET_QK_DIM // 2
    t1, t2 = t[:, :half], t[:, half:]
    return jnp.concatenate([t1 * cos - t2 * sin, t2 * cos + t1 * sin], axis=-1)


def _ret_kernel(cdec_ref, x_ref, g_ref, win_hbm, cos_ref, sin_ref, idec_ref, qdec_ref, kdec_ref,
                gh_ref, wout_hbm, o_ref,
                h_ref, state_ref, q_ref, qd_ref, k_ref, kd_ref, v_ref, og_ref,
                win_ref, wout_ref):
    @pl.when((pl.program_id(0) == 0) & (pl.program_id(1) == 0))
    def _():
        _fill_bf16(win_hbm.at[0], win_ref)
        _fill_bf16(wout_hbm.at[0], wout_ref)

    @pl.when(pl.program_id(1) == 0)
    def _():
        state_ref[...] = jnp.zeros_like(state_ref)

    n_chunks = RET_ROW_TILE // RET_CHUNK
    k_base = RET_HEADS * RET_QK_DIM
    v_base = 2 * RET_HEADS * RET_QK_DIM
    g_base = v_base + RET_HEADS * RET_V_DIM

    def chunk_rows(c):
        return slice(c * RET_CHUNK, (c + 1) * RET_CHUNK)

    def qk_cols(hd):
        return slice(hd * RET_QK_DIM, (hd + 1) * RET_QK_DIM)

    def v_cols(hd):
        return slice(hd * RET_V_DIM, (hd + 1) * RET_V_DIM)

    def w_cols(base, hd, width):
        return slice(base + hd * width, base + (hd + 1) * width)

    h_ref[...] = _rmsnorm(x_ref[0], g_ref[...]).astype(BF16)
    cos = cos_ref[...]
    sin = sin_ref[...]

    for hd in range(RET_HEADS):
        h = h_ref[...]
        q = _rotary(_dot(h, win_ref[:, w_cols(0, hd, RET_QK_DIM)]), cos, sin)
        k = _rotary(_dot(h, win_ref[:, w_cols(k_base, hd, RET_QK_DIM)]) * (RET_QK_DIM ** -0.5),
                    cos, sin)
        q_ref[:, qk_cols(hd)] = q.astype(BF16)
        k_ref[:, qk_cols(hd)] = k.astype(BF16)
        for c in range(n_chunks):
            rows = chunk_rows(c)
            qd_ref[rows, qk_cols(hd)] = (q[rows] * qdec_ref[hd]).astype(BF16)
            kd_ref[rows, qk_cols(hd)] = (k[rows] * kdec_ref[hd]).astype(BF16)
        v_ref[:, v_cols(hd)] = _dot(h, win_ref[:, w_cols(v_base, hd, RET_V_DIM)]).astype(BF16)

    for c in range(n_chunks):
        rows = chunk_rows(c)
        for hd in range(RET_HEADS):
            gate = _dot(h_ref[rows, :], win_ref[:, w_cols(g_base, hd, RET_V_DIM)])
            vb = v_ref[rows, v_cols(hd)]
            scores = _dot_nt(q_ref[rows, qk_cols(hd)], k_ref[rows, qk_cols(hd)]) * idec_ref[hd]
            state = state_ref[hd]
            o = _dot(scores.astype(BF16), vb) + _dot(qd_ref[rows, qk_cols(hd)],
                                                    state.astype(BF16))
            state_ref[hd] = state * cdec_ref[hd] + _dot_tn(kd_ref[rows, qk_cols(hd)], vb)
            o = _rmsnorm(o, gh_ref[:, v_cols(hd)])
            og_ref[rows, v_cols(hd)] = (o * (gate / (1.0 + jnp.exp(-gate)))).astype(BF16)
        o_ref[0, rows, :] = x_ref[0, rows, :] + _dot(og_ref[rows, :], wout_ref[...])


def _retention_tables(seq):
    heads, dk, chunk = RET_HEADS, RET_QK_DIM, RET_CHUNK
    pos = jnp.arange(seq, dtype=F32)
    inv_freq = 1.0 / (ROPE_BASE ** jnp.linspace(0.0, 1.0, dk // 2, dtype=F32))
    ang = pos[:, None] * inv_freq[None, :]
    log_gamma = jnp.log(1.0 - 2.0 ** (-5.0 - jnp.arange(heads, dtype=F32)))
    idx = jnp.arange(chunk, dtype=F32)
    diff = idx[:, None] - idx[None, :]
    inner = jnp.where(diff >= 0, jnp.exp(log_gamma[:, None, None] * jnp.maximum(diff, 0.0)), 0.0)
    qdec = jnp.exp(log_gamma[:, None] * (idx + 1.0))
    kdec = jnp.exp(log_gamma[:, None] * (chunk - 1.0 - idx))
    cdec = jnp.exp(log_gamma * chunk)
    qdec = jnp.broadcast_to(qdec[:, :, None], (heads, chunk, dk))
    kdec = jnp.broadcast_to(kdec[:, :, None], (heads, chunk, dk))
    return jnp.cos(ang), jnp.sin(ang), inner, qdec, kdec, cdec


def _retention_mixer(x, g, w_in, gh, w_out):
    batch, seq, _ = x.shape
    cos, sin, inner, qdec, kdec, cdec = _retention_tables(seq)
    proj = 2 * RET_HEADS * (RET_QK_DIM + RET_V_DIM)
    x_spec = pl.BlockSpec((1, RET_ROW_TILE, D_MODEL), lambda b, j: (b, j, 0))
    rope_spec = pl.BlockSpec((RET_ROW_TILE, RET_QK_DIM // 2), lambda b, j: (j, 0))
    qk_scratch = pltpu.VMEM((RET_ROW_TILE, RET_HEADS * RET_QK_DIM), BF16)
    v_scratch = pltpu.VMEM((RET_ROW_TILE, RET_HEADS * RET_V_DIM), BF16)
    return pl.pallas_call(
        _ret_kernel,
        grid=(batch, seq // RET_ROW_TILE),
        in_specs=[
            pl.BlockSpec(memory_space=pltpu.SMEM),
            x_spec,
            _resident((1, D_MODEL)),
            _HBM,
            rope_spec,
            rope_spec,
            _resident((RET_HEADS, RET_CHUNK, RET_CHUNK)),
            _resident((RET_HEADS, RET_CHUNK, RET_QK_DIM)),
            _resident((RET_HEADS, RET_CHUNK, RET_QK_DIM)),
            _resident((1, RET_HEADS * RET_V_DIM)),
            _HBM,
        ],
        out_specs=x_spec,
        out_shape=jax.ShapeDtypeStruct(x.shape, F32),
        scratch_shapes=[
            pltpu.VMEM((RET_ROW_TILE, D_MODEL), BF16),
            pltpu.VMEM((RET_HEADS, RET_QK_DIM, RET_V_DIM), F32),
            qk_scratch, qk_scratch, qk_scratch, qk_scratch,
            v_scratch,
            v_scratch,
        ] + _weight_scratch((D_MODEL, proj), (RET_HEADS * RET_V_DIM, D_MODEL)),
        compiler_params=_compiler_params(2),
        name="retention_mixer",
    )(cdec, x, g, w_in, cos, sin, inner, qdec, kdec, gh, w_out)


def kernel(x, norm_mix_g, norm_ffn_g, a_w_in, a_b_in, a_v_norm_g, a_w_s, a_b_s, a_w_out,
           b_w_in, b_head_norm_g, b_w_out, mlp_w1, mlp_w2, final_norm_g):
    batch, seq, d = x.shape
    assert d == D_MODEL and norm_mix_g.shape[0] == DEPTH == 2
    assert seq % RET_ROW_TILE == 0
    assert (batch * seq) % FFN_ROW_TILE == 0 and (batch * seq) % GMLP_ROW_TILE == 0
    row = lambda p: p.reshape(1, -1)
    gf = row(final_norm_g)

    x2d = x.reshape(batch * seq, d)
    x2d = _gmlp_mixer(x2d, row(norm_mix_g[0]), a_w_in, row(a_b_in[0]), row(a_v_norm_g[0]),
                      a_w_s[0], a_b_s[0].T, a_w_out)
    x2d = _ffn(x2d, row(norm_ffn_g[0]), mlp_w1, mlp_w2, gf, layer=0, final_norm=False)
    x3d = _retention_mixer(x2d.reshape(batch, seq, d), row(norm_mix_g[1]), b_w_in,
                           row(b_head_norm_g[0]), b_w_out)
    x2d = _ffn(x3d.reshape(batch * seq, d), row(norm_ffn_g[1]), mlp_w1, mlp_w2, gf, layer=1,
               final_norm=True)
    return x2d.reshape(batch, seq, d)
```

```python
import functools
import math

import jax
import jax.numpy as jnp
from jax import lax
from jax.experimental import pallas as pl
from jax.experimental.pallas import tpu as pltpu

D_MODEL = 1024
DEPTH = 2
RMS_EPS = 1e-6
GM_CHUNK = 128
GM_WIDTH = 2 * D_MODEL
GM_GROUPS = 8
GM_GROUP_DIM = GM_WIDTH // GM_GROUPS
RET_QK_DIM = 256
RET_HEADS = D_MODEL // RET_QK_DIM
RET_V_DIM = 2 * D_MODEL // RET_HEADS
ROPE_BASE = 10000.0
D_FF = 4 * D_MODEL

F32 = jnp.float32
BF16 = jnp.bfloat16

V7X_VMEM_LIMIT_BYTES = 56 * 1024 * 1024

GMLP_ROW_TILE = 1024
GMLP_SUB_ROWS = 256
RET_ROW_TILE = 512
FFN_ROW_TILE = 1024
FFN_NORM_ROWS = 256
FF_CHUNK = 1024
RET_CHUNK = 256
WEIGHT_CHUNK_BYTES = 2 * 1024 * 1024


def _rmsnorm(x, g):
    ms = jnp.mean(x * x, axis=-1, keepdims=True)
    return x * lax.rsqrt(ms + RMS_EPS) * g


def _gelu_tanh(x):
    c = -2.0 * (2.0 / math.pi) ** 0.5 * math.log2(math.e)
    return x / (1.0 + jnp.exp2(x * (c + (0.044715 * c) * (x * x))))


def _dot(a, b):
    return jnp.dot(a, b, preferred_element_type=F32)


def _dot_nt(a, b):
    return lax.dot_general(a, b, (((1,), (1,)), ((), ())), preferred_element_type=F32)


def _dot_tn(a, b):
    return lax.dot_general(a, b, (((0,), (0,)), ((), ())), preferred_element_type=F32)


def _resident(shape):
    zeros = (0,) * len(shape)
    return pl.BlockSpec(shape, lambda *_: zeros, pipeline_mode=pl.Buffered(1))


def _fill_bf16(w_hbm, w_vmem):
    rows, cols = w_hbm.shape
    chunk_rows = 1 << ((WEIGHT_CHUNK_BYTES // (cols * 4)).bit_length() - 1)
    n_chunks = rows // chunk_rows
    assert n_chunks * chunk_rows == rows and chunk_rows % 16 == 0

    def fill(stage_ref, sem):
        def chunk_copy(k, slot):
            return pltpu.make_async_copy(w_hbm.at[pl.ds(k * chunk_rows, chunk_rows), :],
                                         stage_ref.at[slot], sem.at[slot])

        chunk_copy(0, 0).start()

        def body(k, carry):
            slot = k % 2

            @pl.when(k + 1 < n_chunks)
            def _():
                chunk_copy(k + 1, 1 - slot).start()

            chunk_copy(k, slot).wait()
            row0 = pl.multiple_of(k * chunk_rows, chunk_rows)
            w_vmem[pl.ds(row0, chunk_rows), :] = stage_ref[slot].astype(BF16)
            return carry

        lax.fori_loop(0, n_chunks, body, 0)

    pl.run_scoped(fill, pltpu.VMEM((2, chunk_rows, cols), F32), pltpu.SemaphoreType.DMA((2,)))


def _weight_scratch(*shapes):
    return [pltpu.VMEM(shape, BF16) for shape in shapes]


_HBM = pl.BlockSpec(memory_space=pl.ANY)


def _compiler_params(n_grid_axes):
    return pltpu.CompilerParams(dimension_semantics=("arbitrary",) * n_grid_axes,
                                vmem_limit_bytes=V7X_VMEM_LIMIT_BYTES)


def _ffn_kernel(x_ref, g_ref, w1_hbm, w2_hbm, gf_ref, o_ref, w1_ref, w2_ref, *, layer, final_norm):
    @pl.when(pl.program_id(0) == 0)
    def _():
        _fill_bf16(w1_hbm.at[layer], w1_ref)
        _fill_bf16(w2_hbm.at[layer], w2_ref)

    x = x_ref[...]
    h_blocks = [_rmsnorm(x[r:r + FFN_NORM_ROWS], g_ref[...]).astype(BF16)
                for r in range(0, FFN_ROW_TILE, FFN_NORM_ROWS)]
    h = jnp.concatenate(h_blocks, axis=0)
    acc = x
    for c in range(D_FF // FF_CHUNK):
        cols = slice(c * FF_CHUNK, (c + 1) * FF_CHUNK)
        if c == 0:
            a = jnp.concatenate([_dot(hb, w1_ref[:, cols]) for hb in h_blocks], axis=0)
        else:
            a = _dot(h, w1_ref[:, cols])
        a = jnp.maximum(a, 0.0)
        a = (a * a).astype(BF16)
        if c + 1 < D_FF // FF_CHUNK:
            acc = acc + _dot(a, w2_ref[cols, :])
    for r in range(0, FFN_ROW_TILE, FFN_NORM_ROWS):
        out = acc[r:r + FFN_NORM_ROWS] + _dot(a[r:r + FFN_NORM_ROWS], w2_ref[cols, :])
        if final_norm:
            out = _rmsnorm(out, gf_ref[...])
        o_ref[r:r + FFN_NORM_ROWS, :] = out


def _ffn(x2d, g, w1, w2, gf, *, layer, final_norm):
    rows = x2d.shape[0]
    row_spec = pl.BlockSpec((FFN_ROW_TILE, D_MODEL), lambda i: (i, 0))
    return pl.pallas_call(
        functools.partial(_ffn_kernel, layer=layer, final_norm=final_norm),
        grid=(rows // FFN_ROW_TILE,),
        in_specs=[
            row_spec,
            _resident((1, D_MODEL)),
            _HBM,
            _HBM,
            _resident((1, D_MODEL)),
        ],
        out_specs=row_spec,
        out_shape=jax.ShapeDtypeStruct(x2d.shape, F32),
        scratch_shapes=_weight_scratch((D_MODEL, D_FF), (D_FF, D_MODEL)),
        compiler_params=_compiler_params(1),
        name="ffn_final" if final_norm else "ffn",
    )(x2d, g, w1, w2, gf)


def _gmlp_kernel(x_ref, g_ref, win_hbm, bin_ref, gv_ref, ws_ref, bst_ref, wout_hbm, o_ref,
                 h_ref, v_ref, win_ref, wout_ref):
    @pl.when(pl.program_id(0) == 0)
    def _():
        _fill_bf16(win_hbm.at[0], win_ref)
        _fill_bf16(wout_hbm.at[0], wout_ref)

    n_sub = GMLP_ROW_TILE // GMLP_SUB_ROWS
    t_idx = lax.broadcasted_iota(jnp.int32, (GM_CHUNK, GM_CHUNK), 0)
    s_idx = lax.broadcasted_iota(jnp.int32, (GM_CHUNK, GM_CHUNK), 1)
    causal = s_idx <= t_idx

    def sub_rows(sub):
        return slice(sub * GMLP_SUB_ROWS, (sub + 1) * GMLP_SUB_ROWS)

    def group_cols(grp):
        return slice(grp * GM_GROUP_DIM, (grp + 1) * GM_GROUP_DIM)

    def project(sub, col0):
        cols = slice(col0, col0 + GM_GROUP_DIM)
        return _dot(h_ref[sub_rows(sub), :], win_ref[:, cols]) + bin_ref[:, cols]

    def normalise(sub):
        rows = sub_rows(sub)
        h_ref[rows, :] = _rmsnorm(x_ref[rows, :], g_ref[...]).astype(BF16)

    def v_group(sub, grp, sumsq):
        v = _gelu_tanh(project(sub, GM_WIDTH + grp * GM_GROUP_DIM))
        v_ref[sub_rows(sub), group_cols(grp)] = v
        return sumsq + jnp.sum(v * v, axis=-1, keepdims=True)

    def gate_group(sub, grp, v_scale):
        cols = group_cols(grp)
        vn = (v_ref[sub_rows(sub), cols] * v_scale * gv_ref[:, cols]).astype(BF16)
        w_causal = jnp.where(causal, ws_ref[grp], 0.0).astype(BF16)
        bias = bst_ref[:, grp:grp + 1]
        return jnp.concatenate(
            [_dot(w_causal, vn[c * GM_CHUNK:(c + 1) * GM_CHUNK]) + bias
             for c in range(GMLP_SUB_ROWS // GM_CHUNK)], axis=0)

    def gated(u_pre, s):
        return (_gelu_tanh(u_pre) * s).astype(BF16)

    zeros = jnp.zeros((GMLP_SUB_ROWS, 1), F32)
    normalise(0)
    sumsq = zeros
    for grp in range(GM_GROUPS):
        sumsq = v_group(0, grp, sumsq)
    for sub in range(n_sub):
        v_scale = lax.rsqrt(sumsq * (1.0 / GM_WIDTH) + RMS_EPS)
        has_next = sub + 1 < n_sub
        if has_next:
            normalise(sub + 1)
        sumsq = zeros
        us = []
        u_pre, s = project(sub, 0), gate_group(sub, 0, v_scale)
        for grp in range(GM_GROUPS):
            if grp + 1 < GM_GROUPS:
                u_pre_next = project(sub, (grp + 1) * GM_GROUP_DIM)
                s_next = gate_group(sub, grp + 1, v_scale)
            if has_next:
                sumsq = v_group(sub + 1, grp, sumsq)
            us.append(gated(u_pre, s))
            u_pre, s = u_pre_next, s_next
        o_ref[sub_rows(sub), :] = x_ref[sub_rows(sub), :] + _dot(jnp.concatenate(us, axis=1),
                                                                 wout_ref[...])


def _gmlp_mixer(x2d, g, w_in, b_in, gv, w_s, b_s_t, w_out):
    rows = x2d.shape[0]
    row_spec = pl.BlockSpec((GMLP_ROW_TILE, D_MODEL), lambda i: (i, 0))
    return pl.pallas_call(
        _gmlp_kernel,
        grid=(rows // GMLP_ROW_TILE,),
        in_specs=[
            row_spec,
            _resident((1, D_MODEL)),
            _HBM,
            _resident((1, 2 * GM_WIDTH)),
            _resident((1, GM_WIDTH)),
            _resident((GM_GROUPS, GM_CHUNK, GM_CHUNK)),
            _resident((GM_CHUNK, GM_GROUPS)),
            _HBM,
        ],
        out_specs=row_spec,
        out_shape=jax.ShapeDtypeStruct(x2d.shape, F32),
        scratch_shapes=[
            pltpu.VMEM((GMLP_ROW_TILE, D_MODEL), BF16),
            pltpu.VMEM((GMLP_ROW_TILE, GM_WIDTH), F32),
        ] + _weight_scratch((D_MODEL, 2 * GM_WIDTH), (GM_WIDTH, D_MODEL)),
        compiler_params=_compiler_params(1),
        name="gmlp_mixer",
    )(x2d, g, w_in, b_in, gv, w_s, b_s_t, w_out)


def _rotary(t, cos, sin):
    half = RET_QK_DIM // 2
    t1, t2 = t[:, :half], t[:, half:]
    return jnp.concatenate([t1 * cos - t2 * sin, t2 * cos + t1 * sin], axis=-1)


def _ret_kernel(cdec_ref, x_ref, g_ref, win_hbm, cos_ref, sin_ref, idec_ref, qdec_ref, kdec_ref,
                gh_ref, wout_hbm, o_ref,
                h_ref, state_ref, q_ref, qd_ref, k_ref, kd_ref, v_ref, og_ref,
                win_ref, wout_ref):
    @pl.when((pl.program_id(0) == 0) & (pl.program_id(1) == 0))
    def _():
        _fill_bf16(win_hbm.at[0], win_ref)
        _fill_bf16(wout_hbm.at[0], wout_ref)

    @pl.when(pl.program_id(1) == 0)
    def _():
        state_ref[...] = jnp.zeros_like(state_ref)

    n_chunks = RET_ROW_TILE // RET_CHUNK
    k_base = RET_HEADS * RET_QK_DIM
    v_base = 2 * RET_HEADS * RET_QK_DIM
    g_base = v_base + RET_HEADS * RET_V_DIM

    def chunk_rows(c):
        return slice(c * RET_CHUNK, (c + 1) * RET_CHUNK)

    def qk_cols(hd):
        return slice(hd * RET_QK_DIM, (hd + 1) * RET_QK_DIM)

    def v_cols(hd):
        return slice(hd * RET_V_DIM, (hd + 1) * RET_V_DIM)

    def w_cols(base, hd, width):
        return slice(base + hd * width, base + (hd + 1) * width)

    h_ref[...] = _rmsnorm(x_ref[0], g_ref[...]).astype(BF16)
    cos = cos_ref[...]
    sin = sin_ref[...]

    for hd in range(RET_HEADS):
        h = h_ref[...]
        q = _rotary(_dot(h, win_ref[:, w_cols(0, hd, RET_QK_DIM)]), cos, sin)
        k = _rotary(_dot(h, win_ref[:, w_cols(k_base, hd, RET_QK_DIM)]) * (RET_QK_DIM ** -0.5),
                    cos, sin)
        q_ref[:, qk_cols(hd)] = q.astype(BF16)
        k_ref[:, qk_cols(hd)] = k.astype(BF16)
        for c in range(n_chunks):
            rows = chunk_rows(c)
            qd_ref[rows, qk_cols(hd)] = (q[rows] * qdec_ref[hd]).astype(BF16)
            kd_ref[rows, qk_cols(hd)] = (k[rows] * kdec_ref[hd]).astype(BF16)
        v_ref[:, v_cols(hd)] = _dot(h, win_ref[:, w_cols(v_base, hd, RET_V_DIM)]).astype(BF16)

    for c in range(n_chunks):
        rows = chunk_rows(c)
        for hd in range(RET_HEADS):
            gate = _dot(h_ref[rows, :], win_ref[:, w_cols(g_base, hd, RET_V_DIM)])
            vb = v_ref[rows, v_cols(hd)]
            scores = _dot_nt(q_ref[rows, qk_cols(hd)], k_ref[rows, qk_cols(hd)]) * idec_ref[hd]
            state = state_ref[hd]
            o = _dot(scores.astype(BF16), vb) + _dot(qd_ref[rows, qk_cols(hd)],
                                                    state.astype(BF16))
            state_ref[hd] = state * cdec_ref[hd] + _dot_tn(kd_ref[rows, qk_cols(hd)], vb)
            o = _rmsnorm(o, gh_ref[:, v_cols(hd)])
            og_ref[rows, v_cols(hd)] = (o * (gate / (1.0 + jnp.exp(-gate)))).astype(BF16)
        o_ref[0, rows, :] = x_ref[0, rows, :] + _dot(og_ref[rows, :], wout_ref[...])


def _retention_tables(seq):
    heads, dk, chunk = RET_HEADS, RET_QK_DIM, RET_CHUNK
    pos = jnp.arange(seq, dtype=F32)
    inv_freq = 1.0 / (ROPE_BASE ** jnp.linspace(0.0, 1.0, dk // 2, dtype=F32))
    ang = pos[:, None] * inv_freq[None, :]
    log_gamma = jnp.log(1.0 - 2.0 ** (-5.0 - jnp.arange(heads, dtype=F32)))
    idx = jnp.arange(chunk, dtype=F32)
    diff = idx[:, None] - idx[None, :]
    inner = jnp.where(diff >= 0, jnp.exp(log_gamma[:, None, None] * jnp.maximum(diff, 0.0)), 0.0)
    qdec = jnp.exp(log_gamma[:, None] * (idx + 1.0))
    kdec = jnp.exp(log_gamma[:, None] * (chunk - 1.0 - idx))
    cdec = jnp.exp(log_gamma * chunk)
    qdec = jnp.broadcast_to(qdec[:, :, None], (heads, chunk, dk))
    kdec = jnp.broadcast_to(kdec[:, :, None], (heads, chunk, dk))
    return jnp.cos(ang), jnp.sin(ang), inner, qdec, kdec, cdec


def _retention_mixer(x, g, w_in, gh, w_out):
    batch, seq, _ = x.shape
    cos, sin, inner, qdec, kdec, cdec = _retention_tables(seq)
    proj = 2 * RET_HEADS * (RET_QK_DIM + RET_V_DIM)
    x_spec = pl.BlockSpec((1, RET_ROW_TILE, D_MODEL), lambda b, j: (b, j, 0))
    rope_spec = pl.BlockSpec((RET_ROW_TILE, RET_QK_DIM // 2), lambda b, j: (j, 0))
    qk_scratch = pltpu.VMEM((RET_ROW_TILE, RET_HEADS * RET_QK_DIM), BF16)
    v_scratch = pltpu.VMEM((RET_ROW_TILE, RET_HEADS * RET_V_DIM), BF16)
    return pl.pallas_call(
        _ret_kernel,
        grid=(batch, seq // RET_ROW_TILE),
        in_specs=[
            pl.BlockSpec(memory_space=pltpu.SMEM),
            x_spec,
            _resident((1, D_MODEL)),
            _HBM,
            rope_spec,
            rope_spec,
            _resident((RET_HEADS, RET_CHUNK, RET_CHUNK)),
            _resident((RET_HEADS, RET_CHUNK, RET_QK_DIM)),
            _resident((RET_HEADS, RET_CHUNK, RET_QK_DIM)),
            _resident((1, RET_HEADS * RET_V_DIM)),
            _HBM,
        ],
        out_specs=x_spec,
        out_shape=jax.ShapeDtypeStruct(x.shape, F32),
        scratch_shapes=[
            pltpu.VMEM((RET_ROW_TILE, D_MODEL), BF16),
            pltpu.VMEM((RET_HEADS, RET_QK_DIM, RET_V_DIM), F32),
            qk_scratch, qk_scratch, qk_scratch, qk_scratch,
            v_scratch,
            v_scratch,
        ] + _weight_scratch((D_MODEL, proj), (RET_HEADS * RET_V_DIM, D_MODEL)),
        compiler_params=_compiler_params(2),
        name="retention_mixer",
    )(cdec, x, g, w_in, cos, sin, inner, qdec, kdec, gh, w_out)


def kernel(x, norm_mix_g, norm_ffn_g, a_w_in, a_b_in, a_v_norm_g, a_w_s, a_b_s, a_w_out,
           b_w_in, b_head_norm_g, b_w_out, mlp_w1, mlp_w2, final_norm_g):
    batch, seq, d = x.shape
    assert d == D_MODEL and norm_mix_g.shape[0] == DEPTH == 2
    assert seq % RET_ROW_TILE == 0
    assert (batch * seq) % FFN_ROW_TILE == 0 and (batch * seq) % GMLP_ROW_TILE == 0
    row = lambda p: p.reshape(1, -1)
    gf = row(final_norm_g)

    x2d = x.reshape(batch * seq, d)
    x2d = _gmlp_mixer(x2d, row(norm_mix_g[0]), a_w_in, row(a_b_in[0]), row(a_v_norm_g[0]),
                      a_w_s[0], a_b_s[0].T, a_w_out)
    x2d = _ffn(x2d, row(norm_ffn_g[0]), mlp_w1, mlp_w2, gf, layer=0, final_norm=False)
    x3d = _retention_mixer(x2d.reshape(batch, seq, d), row(norm_mix_g[1]), b_w_in,
                           row(b_head_norm_g[0]), b_w_out)
    x2d = _ffn(x3d.reshape(batch * seq, d), row(norm_ffn_g[1]), mlp_w1, mlp_w2, gf, layer=1,
               final_norm=True)
    return x2d.reshape(batch, seq, d)
```

```python
import functools
import math

import jax
import jax.numpy as jnp
from jax import lax
from jax.experimental import pallas as pl
from jax.experimental.pallas import tpu as pltpu

D_MODEL = 1024
DEPTH = 2
RMS_EPS = 1e-6
GM_CHUNK = 128
GM_WIDTH = 2 * D_MODEL
GM_GROUPS = 8
GM_GROUP_DIM = GM_WIDTH // GM_GROUPS
RET_QK_DIM = 256
RET_HEADS = D_MODEL // RET_QK_DIM
RET_V_DIM = 2 * D_MODEL // RET_HEADS
ROPE_BASE = 10000.0
D_FF = 4 * D_MODEL

F32 = jnp.float32
BF16 = jnp.bfloat16

V7X_VMEM_LIMIT_BYTES = 56 * 1024 * 1024

GMLP_ROW_TILE = 1024
GMLP_SUB_ROWS = 256
RET_ROW_TILE = 1024
FFN_ROW_TILE = 1024
FFN_NORM_ROWS = 256
FF_CHUNK = 1024
RET_CHUNK = 256
WEIGHT_CHUNK_BYTES = 2 * 1024 * 1024


def _rmsnorm(x, g):
    ms = jnp.mean(x * x, axis=-1, keepdims=True)
    return x * lax.rsqrt(ms + RMS_EPS) * g


def _gelu_tanh(x):
    c = -2.0 * (2.0 / math.pi) ** 0.5 * math.log2(math.e)
    return x / (1.0 + jnp.exp2(x * (c + (0.044715 * c) * (x * x))))


def _dot(a, b):
    return jnp.dot(a, b, preferred_element_type=F32)


def _dot_nt(a, b):
    return lax.dot_general(a, b, (((1,), (1,)), ((), ())), preferred_element_type=F32)


def _dot_tn(a, b):
    return lax.dot_general(a, b, (((0,), (0,)), ((), ())), preferred_element_type=F32)


def _resident(shape):
    zeros = (0,) * len(shape)
    return pl.BlockSpec(shape, lambda *_: zeros, pipeline_mode=pl.Buffered(1))


def _fill_bf16(w_hbm, w_vmem):
    rows, cols = w_hbm.shape
    chunk_rows = 1 << ((WEIGHT_CHUNK_BYTES // (cols * 4)).bit_length() - 1)
    n_chunks = rows // chunk_rows
    assert n_chunks * chunk_rows == rows and chunk_rows % 16 == 0

    def fill(stage_ref, sem):
        def chunk_copy(k, slot):
            return pltpu.make_async_copy(w_hbm.at[pl.ds(k * chunk_rows, chunk_rows), :],
                                         stage_ref.at[slot], sem.at[slot])

        chunk_copy(0, 0).start()

        def body(k, carry):
            slot = k % 2

            @pl.when(k + 1 < n_chunks)
            def _():
                chunk_copy(k + 1, 1 - slot).start()

            chunk_copy(k, slot).wait()
            row0 = pl.multiple_of(k * chunk_rows, chunk_rows)
            w_vmem[pl.ds(row0, chunk_rows), :] = stage_ref[slot].astype(BF16)
            return carry

        lax.fori_loop(0, n_chunks, body, 0)

    pl.run_scoped(fill, pltpu.VMEM((2, chunk_rows, cols), F32), pltpu.SemaphoreType.DMA((2,)))


def _weight_scratch(*shapes):
    return [pltpu.VMEM(shape, BF16) for shape in shapes]


_HBM = pl.BlockSpec(memory_space=pl.ANY)


def _compiler_params(n_grid_axes):
    return pltpu.CompilerParams(dimension_semantics=("arbitrary",) * n_grid_axes,
                                vmem_limit_bytes=V7X_VMEM_LIMIT_BYTES)


def _ffn_kernel(x_ref, g_ref, w1_hbm, w2_hbm, gf_ref, o_ref, w1_ref, w2_ref, *, layer, final_norm):
    @pl.when(pl.program_id(0) == 0)
    def _():
        _fill_bf16(w1_hbm.at[layer], w1_ref)
        _fill_bf16(w2_hbm.at[layer], w2_ref)

    x = x_ref[...]
    h_blocks = [_rmsnorm(x[r:r + FFN_NORM_ROWS], g_ref[...]).astype(BF16)
                for r in range(0, FFN_ROW_TILE, FFN_NORM_ROWS)]
    h = jnp.concatenate(h_blocks, axis=0)
    hidden = []
    for c in range(D_FF // FF_CHUNK):
        cols = slice(c * FF_CHUNK, (c + 1) * FF_CHUNK)
        if c == 0:
            a = jnp.concatenate([_dot(hb, w1_ref[:, cols]) for hb in h_blocks], axis=0)
        else:
            a = _dot(h, w1_ref[:, cols])
        a = jnp.maximum(a, 0.0)
        hidden.append((a * a).astype(BF16))
    a = jnp.concatenate(hidden, axis=1)
    for r in range(0, FFN_ROW_TILE, FFN_NORM_ROWS):
        out = x[r:r + FFN_NORM_ROWS] + _dot(a[r:r + FFN_NORM_ROWS], w2_ref[...])
        if final_norm:
            out = _rmsnorm(out, gf_ref[...])
        o_ref[r:r + FFN_NORM_ROWS, :] = out


def _ffn(x2d, g, w1, w2, gf, *, layer, final_norm):
    rows = x2d.shape[0]
    row_spec = pl.BlockSpec((FFN_ROW_TILE, D_MODEL), lambda i: (i, 0))
    return pl.pallas_call(
        functools.partial(_ffn_kernel, layer=layer, final_norm=final_norm),
        grid=(rows // FFN_ROW_TILE,),
        in_specs=[
            row_spec,
            _resident((1, D_MODEL)),
            _HBM,
            _HBM,
            _resident((1, D_MODEL)),
        ],
        out_specs=row_spec,
        out_shape=jax.ShapeDtypeStruct(x2d.shape, F32),
        scratch_shapes=_weight_scratch((D_MODEL, D_FF), (D_FF, D_MODEL)),
        compiler_params=_compiler_params(1),
        name="ffn_final" if final_norm else "ffn",
    )(x2d, g, w1, w2, gf)


def _gmlp_kernel(x_ref, g_ref, win_hbm, bin_ref, gv_ref, ws_ref, bst_ref, wout_hbm, o_ref,
                 h_ref, v_ref, win_ref, wout_ref):
    @pl.when(pl.program_id(0) == 0)
    def _():
        _fill_bf16(win_hbm.at[0], win_ref)
        _fill_bf16(wout_hbm.at[0], wout_ref)

    n_sub = GMLP_ROW_TILE // GMLP_SUB_ROWS
    t_idx = lax.broadcasted_iota(jnp.int32, (GM_CHUNK, GM_CHUNK), 0)
    s_idx = lax.broadcasted_iota(jnp.int32, (GM_CHUNK, GM_CHUNK), 1)
    causal = s_idx <= t_idx

    def sub_rows(sub):
        return slice(sub * GMLP_SUB_ROWS, (sub + 1) * GMLP_SUB_ROWS)

    def group_cols(grp):
        return slice(grp * GM_GROUP_DIM, (grp + 1) * GM_GROUP_DIM)

    def project(sub, col0):
        cols = slice(col0, col0 + GM_GROUP_DIM)
        return _dot(h_ref[sub_rows(sub), :], win_ref[:, cols]) + bin_ref[:, cols]

    def normalise(sub):
        rows = sub_rows(sub)
        h_ref[rows, :] = _rmsnorm(x_ref[rows, :], g_ref[...]).astype(BF16)

    def v_group(sub, grp, sumsq):
        v = _gelu_tanh(project(sub, GM_WIDTH + grp * GM_GROUP_DIM))
        v_ref[sub_rows(sub), group_cols(grp)] = v
        return sumsq + jnp.sum(v * v, axis=-1, keepdims=True)

    def gate_group(sub, grp, v_scale):
        cols = group_cols(grp)
        vn = (v_ref[sub_rows(sub), cols] * v_scale * gv_ref[:, cols]).astype(BF16)
        w_causal = jnp.where(causal, ws_ref[grp], 0.0).astype(BF16)
        bias = bst_ref[:, grp:grp + 1]
        return jnp.concatenate(
            [_dot(w_causal, vn[c * GM_CHUNK:(c + 1) * GM_CHUNK]) + bias
             for c in range(GMLP_SUB_ROWS // GM_CHUNK)], axis=0)

    def gated(u_pre, s):
        return (_gelu_tanh(u_pre) * s).astype(BF16)

    zeros = jnp.zeros((GMLP_SUB_ROWS, 1), F32)
    normalise(0)
    sumsq = zeros
    for grp in range(GM_GROUPS):
        sumsq = v_group(0, grp, sumsq)
    for sub in range(n_sub):
        v_scale = lax.rsqrt(sumsq * (1.0 / GM_WIDTH) + RMS_EPS)
        has_next = sub + 1 < n_sub
        if has_next:
            normalise(sub + 1)
        sumsq = zeros
        us = []
        u_pre, s = project(sub, 0), gate_group(sub, 0, v_scale)
        for grp in range(GM_GROUPS):
            if grp + 1 < GM_GROUPS:
                u_pre_next = project(sub, (grp + 1) * GM_GROUP_DIM)
                s_next = gate_group(sub, grp + 1, v_scale)
            if has_next:
                sumsq = v_group(sub + 1, grp, sumsq)
            us.append(gated(u_pre, s))
            u_pre, s = u_pre_next, s_next
        o_ref[sub_rows(sub), :] = x_ref[sub_rows(sub), :] + _dot(jnp.concatenate(us, axis=1),
                                                                 wout_ref[...])


def _gmlp_mixer(x2d, g, w_in, b_in, gv, w_s, b_s_t, w_out):
    rows = x2d.shape[0]
    row_spec = pl.BlockSpec((GMLP_ROW_TILE, D_MODEL), lambda i: (i, 0))
    return pl.pallas_call(
        _gmlp_kernel,
        grid=(rows // GMLP_ROW_TILE,),
        in_specs=[
            row_spec,
            _resident((1, D_MODEL)),
            _HBM,
            _resident((1, 2 * GM_WIDTH)),
            _resident((1, GM_WIDTH)),
            _resident((GM_GROUPS, GM_CHUNK, GM_CHUNK)),
            _resident((GM_CHUNK, GM_GROUPS)),
            _HBM,
        ],
        out_specs=row_spec,
        out_shape=jax.ShapeDtypeStruct(x2d.shape, F32),
        scratch_shapes=[
            pltpu.VMEM((GMLP_ROW_TILE, D_MODEL), BF16),
            pltpu.VMEM((GMLP_ROW_TILE, GM_WIDTH), F32),
        ] + _weight_scratch((D_MODEL, 2 * GM_WIDTH), (GM_WIDTH, D_MODEL)),
        compiler_params=_compiler_params(1),
        name="gmlp_mixer",
    )(x2d, g, w_in, b_in, gv, w_s, b_s_t, w_out)


def _rotary(t, cos, sin):
    half = RET_QK_DIM // 2
    t1, t2 = t[:, :half], t[:, half:]
    return jnp.concatenate([t1 * cos - t2 * sin, t2 * cos + t1 * sin], axis=-1)


def _ret_kernel(cdec_ref, x_ref, g_ref, win_hbm, cos_ref, sin_ref, idec_ref, qdec_ref, kdec_ref,
                gh_ref, wout_hbm, o_ref,
                h_ref, state_ref, q_ref, qd_ref, k_ref, kd_ref, v_ref, og_ref,
                win_ref, wout_ref):
    @pl.when((pl.program_id(0) == 0) & (pl.program_id(1) == 0))
    def _():
        _fill_bf16(win_hbm.at[0], win_ref)
        _fill_bf16(wout_hbm.at[0], wout_ref)

    @pl.when(pl.program_id(1) == 0)
    def _():
        state_ref[...] = jnp.zeros_like(state_ref)

    n_chunks = RET_ROW_TILE // RET_CHUNK
    k_base = RET_HEADS * RET_QK_DIM
    v_base = 2 * RET_HEADS * RET_QK_DIM
    g_base = v_base + RET_HEADS * RET_V_DIM

    def chunk_rows(c):
        return slice(c * RET_CHUNK, (c + 1) * RET_CHUNK)

    def qk_cols(hd):
        return slice(hd * RET_QK_DIM, (hd + 1) * RET_QK_DIM)

    def v_cols(hd):
        return slice(hd * RET_V_DIM, (hd + 1) * RET_V_DIM)

    def w_cols(base, hd, width):
        return slice(base + hd * width, base + (hd + 1) * width)

    for c in range(n_chunks):
        rows = chunk_rows(c)
        slot = c % 2
        h_ref[slot] = _rmsnorm(x_ref[0, rows, :], g_ref[...]).astype(BF16)
        cos = cos_ref[rows, :]
        sin = sin_ref[rows, :]

        for hd in range(RET_HEADS):
            h = h_ref[slot]
            q = _rotary(_dot(h, win_ref[:, w_cols(0, hd, RET_QK_DIM)]), cos, sin)
            k = _rotary(_dot(h, win_ref[:, w_cols(k_base, hd, RET_QK_DIM)])
                        * (RET_QK_DIM ** -0.5), cos, sin)
            q_ref[slot, :, qk_cols(hd)] = q.astype(BF16)
            k_ref[slot, :, qk_cols(hd)] = k.astype(BF16)
            qd_ref[slot, :, qk_cols(hd)] = (q * qdec_ref[hd]).astype(BF16)
            kd_ref[slot, :, qk_cols(hd)] = (k * kdec_ref[hd]).astype(BF16)
            v_ref[slot, :, v_cols(hd)] = _dot(h, win_ref[:, w_cols(v_base, hd, RET_V_DIM)]
                                              ).astype(BF16)

        for hd in range(RET_HEADS):
            gate = _dot(h_ref[slot], win_ref[:, w_cols(g_base, hd, RET_V_DIM)])
            vb = v_ref[slot, :, v_cols(hd)]
            scores = _dot_nt(q_ref[slot, :, qk_cols(hd)], k_ref[slot, :, qk_cols(hd)]) * idec_ref[hd]
            state = state_ref[hd]
            o = _dot(scores.astype(BF16), vb) + _dot(qd_ref[slot, :, qk_cols(hd)],
                                                    state.astype(BF16))
            state_ref[hd] = state * cdec_ref[hd] + _dot_tn(kd_ref[slot, :, qk_cols(hd)], vb)
            o = _rmsnorm(o, gh_ref[:, v_cols(hd)])
            og_ref[slot, :, v_cols(hd)] = (o * (gate / (1.0 + jnp.exp(-gate)))).astype(BF16)
        o_ref[0, rows, :] = x_ref[0, rows, :] + _dot(og_ref[slot], wout_ref[...])


def _retention_tables(seq):
    heads, dk, chunk = RET_HEADS, RET_QK_DIM, RET_CHUNK
    pos = jnp.arange(seq, dtype=F32)
    inv_freq = 1.0 / (ROPE_BASE ** jnp.linspace(0.0, 1.0, dk // 2, dtype=F32))
    ang = pos[:, None] * inv_freq[None, :]
    log_gamma = jnp.log(1.0 - 2.0 ** (-5.0 - jnp.arange(heads, dtype=F32)))
    idx = jnp.arange(chunk, dtype=F32)
    diff = idx[:, None] - idx[None, :]
    inner = jnp.where(diff >= 0, jnp.exp(log_gamma[:, None, None] * jnp.maximum(diff, 0.0)), 0.0)
    qdec = jnp.exp(log_gamma[:, None] * (idx + 1.0))
    kdec = jnp.exp(log_gamma[:, None] * (chunk - 1.0 - idx))
    cdec = jnp.exp(log_gamma * chunk)
    qdec = jnp.broadcast_to(qdec[:, :, None], (heads, chunk, dk))
    kdec = jnp.broadcast_to(kdec[:, :, None], (heads, chunk, dk))
    return jnp.cos(ang), jnp.sin(ang), inner, qdec, kdec, cdec


def _retention_mixer(x, g, w_in, gh, w_out):
    batch, seq, _ = x.shape
    cos, sin, inner, qdec, kdec, cdec = _retention_tables(seq)
    proj = 2 * RET_HEADS * (RET_QK_DIM + RET_V_DIM)
    x_spec = pl.BlockSpec((1, RET_ROW_TILE, D_MODEL), lambda b, j: (b, j, 0))
    rope_spec = pl.BlockSpec((RET_ROW_TILE, RET_QK_DIM // 2), lambda b, j: (j, 0))
    qk_scratch = pltpu.VMEM((2, RET_CHUNK, RET_HEADS * RET_QK_DIM), BF16)
    v_scratch = pltpu.VMEM((2, RET_CHUNK, RET_HEADS * RET_V_DIM), BF16)
    return pl.pallas_call(
        _ret_kernel,
        grid=(batch, seq // RET_ROW_TILE),
        in_specs=[
            pl.BlockSpec(memory_space=pltpu.SMEM),
            x_spec,
            _resident((1, D_MODEL)),
            _HBM,
            rope_spec,
            rope_spec,
            _resident((RET_HEADS, RET_CHUNK, RET_CHUNK)),
            _resident((RET_HEADS, RET_CHUNK, RET_QK_DIM)),
            _resident((RET_HEADS, RET_CHUNK, RET_QK_DIM)),
            _resident((1, RET_HEADS * RET_V_DIM)),
            _HBM,
        ],
        out_specs=x_spec,
        out_shape=jax.ShapeDtypeStruct(x.shape, F32),
        scratch_shapes=[
            pltpu.VMEM((2, RET_CHUNK, D_MODEL), BF16),
            pltpu.VMEM((RET_HEADS, RET_QK_DIM, RET_V_DIM), F32),
            qk_scratch, qk_scratch, qk_scratch, qk_scratch,
            v_scratch,
            v_scratch,
        ] + _weight_scratch((D_MODEL, proj), (RET_HEADS * RET_V_DIM, D_MODEL)),
        compiler_params=_compiler_params(2),
        name="retention_mixer",
    )(cdec, x, g, w_in, cos, sin, inner, qdec, kdec, gh, w_out)


def kernel(x, norm_mix_g, norm_ffn_g, a_w_in, a_b_in, a_v_norm_g, a_w_s, a_b_s, a_w_out,
           b_w_in, b_head_norm_g, b_w_out, mlp_w1, mlp_w2, final_norm_g):
    batch, seq, d = x.shape
    assert d == D_MODEL and norm_mix_g.shape[0] == DEPTH == 2
    assert seq % RET_ROW_TILE == 0
    assert (batch * seq) % FFN_ROW_TILE == 0 and (batch * seq) % GMLP_ROW_TILE == 0
    row = lambda p: p.reshape(1, -1)
    gf = row(final_norm_g)

    x2d = x.reshape(batch * seq, d)
    x2d = _gmlp_mixer(x2d, row(norm_mix_g[0]), a_w_in, row(a_b_in[0]), row(a_v_norm_g[0]),
                      a_w_s[0], a_b_s[0].T, a_w_out)
    x2d = _ffn(x2d, row(norm_ffn_g[0]), mlp_w1, mlp_w2, gf, layer=0, final_norm=False)
    x3d = _retention_mixer(x2d.reshape(batch, seq, d), row(norm_mix_g[1]), b_w_in,
                           row(b_head_norm_g[0]), b_w_out)
    x2d = _ffn(x3d.reshape(batch * seq, d), row(norm_ffn_g[1]), mlp_w1, mlp_w2, gf, layer=1,
               final_norm=True)
    return x2d.reshape(batch, seq, d)
```
